```python
import math
import jax, jax.numpy as jnp
from jax import lax
import numpy as np

D_MODEL = 1024
BATCH = 8
SEQ = 4096
DEPTH = 1

HEAD_DIM = 64
MLA_HEADS = 8
MLA_Q_RANK = 256
MLA_KV_RANK = 128
MLA_NOPE_DIM = 64
MLA_ROPE_DIM = 32
MLA_V_DIM = HEAD_DIM
ROPE_THETA = 10000.0
DIL_HEADS = 8
DIL_PAIRS = ((128, 1), (512, 4), (2048, 16))
DIL_BLOCK = 128
DIL_WIDTH = DIL_HEADS * HEAD_DIM
MLA_WIDTH = MLA_HEADS * MLA_V_DIM
MIX_WIDTH = MLA_WIDTH + DIL_WIDTH
IN_SPLITS = (MLA_Q_RANK, MLA_KV_RANK, MLA_ROPE_DIM, DIL_WIDTH, DIL_WIDTH, DIL_WIDTH)
IN_WIDTH = sum(IN_SPLITS)
D_FF = 2816
CONV_WIDTH = 3
Q_BLOCK = 128
DN_ALPHA = (2.0 * DEPTH) ** 0.25
DN_BETA = (8.0 * DEPTH) ** -0.25
LN_EPS = 1e-5
RMS_EPS = 1e-6

kernel_name = "hybrid_mla_dilated_swa_convffn_deepnorm"


def layer_norm(x, g, b):
    xf = x.astype(jnp.float32)
    mu = jnp.mean(xf, axis=-1, keepdims=True)
    var = jnp.mean(jnp.square(xf - mu), axis=-1, keepdims=True)
    y = (xf - mu) * lax.rsqrt(var + LN_EPS) * g.astype(jnp.float32) + b.astype(jnp.float32)
    return y.astype(x.dtype)


def rms_norm(x, g):
    xf = x.astype(jnp.float32)
    y = xf * lax.rsqrt(jnp.mean(jnp.square(xf), axis=-1, keepdims=True) + RMS_EPS)
    return (y * g.astype(jnp.float32)).astype(x.dtype)


def apply_rope(x, pos):
    half = x.shape[-1] // 2
    freqs = ROPE_THETA ** (-jnp.arange(half, dtype=jnp.float32) / half)
    ang = pos.astype(jnp.float32)[:, None] * freqs[None, :]
    cos = jnp.cos(ang)[None, :, None, :]
    sin = jnp.sin(ang)[None, :, None, :]
    xf = x.astype(jnp.float32)
    x1, x2 = xf[..., :half], xf[..., half:]
    out = jnp.concatenate([x1 * cos - x2 * sin, x1 * sin + x2 * cos], axis=-1)
    return out.astype(x.dtype)


def alibi_slopes(n):
    return 2.0 ** (-8.0 * jnp.arange(1, n + 1, dtype=jnp.float32) / n)


def mla_attention(c_q, c_kv, k_rope, g_cq, g_ckv, w_uq, w_uk, w_uv):
    B, S, _ = c_q.shape
    pos = jnp.arange(S)
    c_q = rms_norm(c_q, g_cq)
    c_kv = rms_norm(c_kv, g_ckv)
    q = jnp.einsum('bsr,rhe->bshe', c_q, w_uq)
    q_nope, q_rope = q[..., :MLA_NOPE_DIM], q[..., MLA_NOPE_DIM:]
    k_nope = jnp.einsum('bsr,rhe->bshe', c_kv, w_uk)
    v = jnp.einsum('bsr,rhe->bshe', c_kv, w_uv)
    q_rope = apply_rope(q_rope, pos)
    k_rope = apply_rope(k_rope[:, :, None, :], pos)
    qf = jnp.concatenate([q_nope, q_rope], axis=-1)
    kf = jnp.concatenate([k_nope, jnp.broadcast_to(k_rope, k_nope.shape[:3] + (MLA_ROPE_DIM,))], axis=-1)
    scale = 1.0 / math.sqrt(MLA_NOPE_DIM + MLA_ROPE_DIM)
    nb = S // Q_BLOCK
    q_blocks = qf.reshape(B, nb, Q_BLOCK, MLA_HEADS, -1).transpose(1, 0, 2, 3, 4)
    k_pos = jnp.arange(S)

    def one_block(args):
        qb, bi = args
        s = jnp.einsum('bqhe,bkhe->bhqk', qb, kf).astype(jnp.float32) * scale
        q_pos = bi * Q_BLOCK + jnp.arange(Q_BLOCK)
        causal = q_pos[:, None] >= k_pos[None, :]
        s = jnp.where(causal[None, None], s, -jnp.inf)
        p = jax.nn.softmax(s, axis=-1).astype(v.dtype)
        return jnp.einsum('bhqk,bkhe->bqhe', p, v)

    o = lax.map(one_block, (q_blocks, jnp.arange(nb)))
    return o.transpose(1, 0, 2, 3, 4).reshape(B, S, MLA_HEADS * MLA_V_DIM)


def dilated_branch(q, k, v, slopes, window, dil):
    B, S, H, E = q.shape
    n_back = window // dil
    blk = DIL_BLOCK
    L = -(-S // (dil * blk)) * dil * blk
    M = L // dil
    nb = M // blk

    def to_sub(a):
        a = jnp.pad(a, ((0, 0), (0, L - S), (0, 0), (0, 0)))
        a = a.reshape(B, M, dil, H, E).transpose(0, 2, 1, 3, 4)
        return a.reshape(B, dil, nb, blk, H, E)

    def with_prev(ab):
        prev = jnp.pad(ab, ((0, 0), (0, 0), (1, 0), (0, 0), (0, 0), (0, 0)))[:, :, :-1]
        return jnp.concatenate([prev, ab], axis=3)

    qb = to_sub(q)
    kb = with_prev(to_sub(k))
    vb = with_prev(to_sub(v))
    s = jnp.einsum('bdnqhe,bdnkhe->bdnhqk', qb, kb).astype(jnp.float32) / math.sqrt(E)
    jq = jnp.arange(nb)[:, None] * blk + jnp.arange(blk)[None, :]
    jk = jnp.arange(nb)[:, None] * blk - blk + jnp.arange(2 * blk)[None, :]
    off = jq[:, :, None] - jk[:, None, :]
    valid = (off >= 0) & (off <= n_back) & (jk[:, None, :] >= 0)
    dist = (off * dil).astype(jnp.float32)
    bias = -slopes[None, :, None, None] * dist[:, None]
    s = jnp.where(valid[:, None], s + bias, -jnp.inf)
    lse = jax.nn.logsumexp(s, axis=-1)
    p = jnp.exp(s - lse[..., None]).astype(v.dtype)
    o = jnp.einsum('bdnhqk,bdnkhe->bdnqhe', p, vb)
    o = o.reshape(B, dil, M, H, E).transpose(0, 2, 1, 3, 4).reshape(B, L, H, E)[:, :S]
    lse = lse.transpose(0, 1, 2, 4, 3).reshape(B, dil, M, H).transpose(0, 2, 1, 3).reshape(B, L, H)[:, :S]
    return o, lse


def dilated_attention(q, k, v):
    B, S, H, E = q.shape
    slopes = alibi_slopes(H)
    outs, lses = [], []
    for window, dil in DIL_PAIRS:
        o, lse = dilated_branch(q, k, v, slopes, window, dil)
        outs.append(o.astype(jnp.float32))
        lses.append(lse)
    w = jax.nn.softmax(jnp.stack(lses, axis=0), axis=0)
    o = jnp.sum(w[..., None] * jnp.stack(outs, axis=0), axis=0)
    return o.astype(q.dtype).reshape(B, S, H * E)


def conv_gated_ffn(x, w_up, conv_w, conv_b, w_down):
    S = x.shape[1]
    u = x @ w_up
    y = conv_b
    for j in range(CONV_WIDTH):
        shift = CONV_WIDTH - 1 - j
        us = jnp.pad(u, ((0, 0), (shift, 0), (0, 0)))[:, :S] if shift else u
        y = y + conv_w[j] * us
    a, g = y[..., :D_FF], y[..., D_FF:]
    return (jax.nn.gelu(g) * a) @ w_down


def setup_inputs(seed: int = 0) -> dict:
    key = jax.random.key(seed)
    ks = jax.random.split(key, 17)
    f32 = jnp.float32
    x = jax.random.normal(ks[0], (BATCH, SEQ, D_MODEL), f32)
    col_scale = jnp.concatenate([jnp.ones((IN_WIDTH - DIL_WIDTH,), f32),
                                 jnp.full((DIL_WIDTH,), DN_BETA, f32)])
    w_in = jax.random.normal(ks[1], (D_MODEL, IN_WIDTH), f32) * D_MODEL ** -0.5 * col_scale
    g_cq = 1.0 + 0.02 * jax.random.normal(ks[2], (MLA_Q_RANK,), f32)
    g_ckv = 1.0 + 0.02 * jax.random.normal(ks[3], (MLA_KV_RANK,), f32)
    w_uq = jax.random.normal(ks[4], (MLA_Q_RANK, MLA_HEADS, MLA_NOPE_DIM + MLA_ROPE_DIM), f32) * MLA_Q_RANK ** -0.5
    w_uk = jax.random.normal(ks[5], (MLA_KV_RANK, MLA_HEADS, MLA_NOPE_DIM), f32) * MLA_KV_RANK ** -0.5
    w_uv = jax.random.normal(ks[6], (MLA_KV_RANK, MLA_HEADS, MLA_V_DIM), f32) * MLA_KV_RANK ** -0.5 * DN_BETA
    w_o = jax.random.normal(ks[7], (MIX_WIDTH, D_MODEL), f32) * MIX_WIDTH ** -0.5 * DN_BETA
    ln1_g = 1.0 + 0.02 * jax.random.normal(ks[8], (D_MODEL,), f32)
    ln1_b = 0.02 * jax.random.normal(ks[9], (D_MODEL,), f32)
    w_up = jax.random.normal(ks[10], (D_MODEL, 2 * D_FF), f32) * D_MODEL ** -0.5 * DN_BETA
    conv_w = jax.random.normal(ks[11], (CONV_WIDTH, 2 * D_FF), f32) * CONV_WIDTH ** -0.5
    conv_b = 0.01 * jax.random.normal(ks[12], (2 * D_FF,), f32)
    w_down = jax.random.normal(ks[13], (D_FF, D_MODEL), f32) * D_FF ** -0.5 * DN_BETA
    ln2_g = 1.0 + 0.02 * jax.random.normal(ks[14], (D_MODEL,), f32)
    ln2_b = 0.02 * jax.random.normal(ks[15], (D_MODEL,), f32)
    return {"x": x, "w_in": w_in, "g_cq": g_cq, "g_ckv": g_ckv, "w_uq": w_uq,
            "w_uk": w_uk, "w_uv": w_uv, "w_o": w_o, "ln1_g": ln1_g, "ln1_b": ln1_b,
            "w_up": w_up, "conv_w": conv_w, "conv_b": conv_b, "w_down": w_down,
            "ln2_g": ln2_g, "ln2_b": ln2_b}


def reference(x, w_in, g_cq, g_ckv, w_uq, w_uk, w_uv, w_o, ln1_g, ln1_b,
              w_up, conv_w, conv_b, w_down, ln2_g, ln2_b):
    B, S, _ = x.shape
    for _layer in range(DEPTH):
        h = x @ w_in
        idx = np.cumsum(IN_SPLITS)[:-1].tolist()
        c_q, c_kv, k_rope, q_d, k_d, v_d = jnp.split(h, idx, axis=-1)
        o_mla = mla_attention(c_q, c_kv, k_rope, g_cq, g_ckv, w_uq, w_uk, w_uv)
        o_dil = dilated_attention(q_d.reshape(B, S, DIL_HEADS, HEAD_DIM),
                                  k_d.reshape(B, S, DIL_HEADS, HEAD_DIM),
                                  v_d.reshape(B, S, DIL_HEADS, HEAD_DIM))
        mix = jnp.concatenate([o_mla, o_dil], axis=-1) @ w_o
        x = layer_norm(DN_ALPHA * x + mix, ln1_g, ln1_b)
        ffn = conv_gated_ffn(x, w_up, conv_w, conv_b, w_down)
        x = layer_norm(DN_ALPHA * x + ffn, ln2_g, ln2_b)
    return x
```

```python
import functools
import math

import jax
import jax.numpy as jnp
from jax import lax
from jax.experimental import pallas as pl
from jax.experimental.pallas import tpu as pltpu

D_MODEL = 1024
HEAD_DIM = 64
MLA_HEADS = 8
MLA_Q_RANK = 256
MLA_KV_RANK = 128
MLA_NOPE_DIM = 64
MLA_ROPE_DIM = 32
ROPE_THETA = 10000.0
DIL_HEADS = 8
DIL_PAIRS = ((128, 1), (512, 4), (2048, 16))
DIL_BLOCK = 128
DIL_WIDTH = DIL_HEADS * HEAD_DIM
D_FF = 2816
CONV_WIDTH = 3
DEPTH = 1
DN_ALPHA = (2.0 * DEPTH) ** 0.25
LN_EPS = 1e-5
RMS_EPS = 1e-6

LANES = 128
BF16_ROWS = 16
VMEM_LIMIT = 56 * 1024 * 1024

PROJ_ROWS = 512
MLA_BLOCK = 512
FFN_ROWS = 512
FFN_CHUNK = 256
NEG_BIG = 1e30

BF16 = jnp.bfloat16
F32 = jnp.float32


def _dot(a, b):
    return jnp.dot(a, b, preferred_element_type=F32)


def _dot_nt(a, b):
    return lax.dot_general(a, b, (((1,), (1,)), ((), ())), preferred_element_type=F32)


def _params(*sem):
    return pltpu.CompilerParams(dimension_semantics=sem, vmem_limit_bytes=VMEM_LIMIT)


def _const_spec(shape):
    zeros = (0,) * len(shape)
    return pl.BlockSpec(shape, lambda *_: zeros, pipeline_mode=pl.Buffered(1))


def _rope_table_kernel(freq_ref, out_ref, *, scale):
    rows = out_ref.shape[1]
    pos = lax.broadcasted_iota(jnp.int32, (rows, LANES), 0).astype(F32)
    ang = pos * freq_ref[...]
    c = jnp.cos(ang)
    s = jnp.sin(ang)
    out_ref[0] = c
    out_ref[1] = s
    out_ref[2] = c * scale
    out_ref[3] = s * scale


def _rope_tables(seq, freq_lanes, scale):
    return pl.pallas_call(
        functools.partial(_rope_table_kernel, scale=scale),
        out_shape=jax.ShapeDtypeStruct((4, seq, LANES), F32),
        name="rope_tables",
    )(freq_lanes)


def _rms(x, g):
    ms = jnp.mean(x * x, axis=-1, keepdims=True)
    return x * lax.rsqrt(ms + RMS_EPS) * g


def _proj_kernel(x_ref, w_in_ref, gq_ref, gkv_ref, wq_ref, wkv_ref, tab_ref, vone_ref,
                 q_ref, k_ref, v_ref, qd_ref, kd_ref, vd_ref):
    xb = x_ref[0].astype(BF16)
    h = _dot(xb, w_in_ref[...])
    cq = _rms(h[:, :MLA_Q_RANK], gq_ref[...]).astype(BF16)
    ckv = _rms(h[:, MLA_Q_RANK:MLA_Q_RANK + MLA_KV_RANK], gkv_ref[...]).astype(BF16)
    kr = h[:, 3 * LANES:4 * LANES]
    cos, sin, cos_q, sin_q = tab_ref[0], tab_ref[1], tab_ref[2], tab_ref[3]

    k_plain = pltpu.roll(kr, 64, 1)
    k_swap = pltpu.roll(kr, 80, 1) - pltpu.roll(kr, 48, 1)
    k_rope = k_plain * cos + k_swap * sin

    q2 = _dot(cq, wq_ref[...])
    kv = _dot(ckv, wkv_ref[...])
    hw = MLA_HEADS * LANES
    for hd in range(MLA_HEADS):
        lo = hd * LANES
        qh = q2[:, lo:lo + LANES] * cos_q + q2[:, hw + lo:hw + lo + LANES] * sin_q
        q_ref[0, hd] = qh.astype(BF16)
        k_ref[0, hd] = (kv[:, lo:lo + LANES] + k_rope).astype(BF16)
        v_ref[0, hd] = (kv[:, hw + lo:hw + lo + LANES] + vone_ref[:, lo:lo + LANES]).astype(BF16)

    base = 4 * LANES
    qd_ref[0] = (h[:, base:base + DIL_WIDTH] * (1.0 / math.sqrt(HEAD_DIM))).astype(BF16)
    kd_ref[0] = h[:, base + DIL_WIDTH:base + 2 * DIL_WIDTH].astype(BF16)
    vd_ref[0] = h[:, base + 2 * DIL_WIDTH:base + 3 * DIL_WIDTH].astype(BF16)


def _projection(x, w_in_p, gq, gkv, wq2, wkv, tables, vone):
    B, S, _ = x.shape
    rows = min(PROJ_ROWS, S)
    hw = MLA_HEADS * LANES
    head_shape = jax.ShapeDtypeStruct((B, MLA_HEADS, S, LANES), BF16)
    dil_shape = jax.ShapeDtypeStruct((B, S, DIL_WIDTH), BF16)
    head_spec = pl.BlockSpec((1, MLA_HEADS, rows, LANES), lambda b, i: (b, 0, i, 0))
    dil_spec = pl.BlockSpec((1, rows, DIL_WIDTH), lambda b, i: (b, i, 0))
    return pl.pallas_call(
        _proj_kernel,
        grid=(B, S // rows),
        in_specs=[
            pl.BlockSpec((1, rows, D_MODEL), lambda b, i: (b, i, 0)),
            _const_spec(w_in_p.shape),
            _const_spec(gq.shape),
            _const_spec(gkv.shape),
            _const_spec(wq2.shape),
            _const_spec(wkv.shape),
            pl.BlockSpec((4, rows, LANES), lambda b, i: (0, i, 0)),
            _const_spec(vone.shape),
        ],
        out_specs=[head_spec, head_spec, head_spec, dil_spec, dil_spec, dil_spec],
        out_shape=[head_shape, head_shape, head_shape, dil_shape, dil_shape, dil_shape],
        compiler_params=_params("parallel", "parallel"),
        name="in_projection",
    )(x, w_in_p, gq, gkv, wq2, wkv, tables, vone)


def _mla_kernel(q_ref, k_ref, v_ref, o_ref, acc_ref, m_ref, *, blk):
    qi = pl.program_id(2)
    lane = lax.broadcasted_iota(jnp.int32, (blk, LANES), 1)
    row = lax.broadcasted_iota(jnp.int32, (blk, blk), 0)
    col = lax.broadcasted_iota(jnp.int32, (blk, blk), 1)
    outs = []
    for hh in range(2):
        q = q_ref[0, hh]
        acc_ref[...] = jnp.zeros_like(acc_ref)
        m_ref[...] = jnp.full_like(m_ref, -NEG_BIG)

        def step(j, masked, hh=hh, q=q):
            start = pl.multiple_of(j * blk, blk)
            k = k_ref[0, hh, pl.ds(start, blk), :]
            v = v_ref[0, hh, pl.ds(start, blk), :]
            s = _dot_nt(q, k)
            if masked:
                s = jnp.where(row >= col, s, -NEG_BIG)
            m_old = m_ref[...]
            m_new = jnp.maximum(m_old, jnp.max(s, axis=-1, keepdims=True))
            p = jnp.exp(s - m_new)
            acc_ref[...] = acc_ref[...] * jnp.exp(m_old - m_new) + _dot(p.astype(BF16), v)
            m_ref[...] = m_new

        def body(j, carry):
            step(j, False)
            return carry

        lax.fori_loop(0, qi, body, 0)
        step(qi, True)
        acc = acc_ref[...]
        denom = acc[:, 64:65] if hh == 0 else acc[:, 0:1]
        outs.append(acc / denom)
    o_ref[0] = jnp.where(lane < HEAD_DIM, outs[0], outs[1]).astype(o_ref.dtype)


def _mla_attention(q, k, v):
    B, H, S, _ = q.shape
    blk = min(MLA_BLOCK, S)
    kv_spec = pl.BlockSpec((1, 2, S, LANES), lambda b, hp, i: (b, hp, 0, 0))
    return pl.pallas_call(
        functools.partial(_mla_kernel, blk=blk),
        grid=(B, H // 2, S // blk),
        in_specs=[pl.BlockSpec((1, 2, blk, LANES), lambda b, hp, i: (b, hp, i, 0)), kv_spec, kv_spec],
        out_specs=pl.BlockSpec((1, blk, LANES), lambda b, hp, i: (b, i, hp)),
        out_shape=jax.ShapeDtypeStruct((B, S, H * HEAD_DIM), BF16),
        scratch_shapes=[pltpu.VMEM((blk, LANES), F32), pltpu.VMEM((blk, 1), F32)],
        compiler_params=_params("parallel", "parallel", "arbitrary"),
        name="mla_attention",
    )(q, k, v)


def _dil_scores(q_ref, kp_ref, kc_ref, vp_ref, vc_ref, n, dil):
    blk = DIL_BLOCK
    q = q_ref[0]
    k = jnp.concatenate([kp_ref[0], kc_ref[0]], axis=0)
    v = jnp.concatenate([vp_ref[0], vc_ref[0]], axis=0)
    row = lax.broadcasted_iota(jnp.int32, (blk, 2 * blk), 0)
    col = lax.broadcasted_iota(jnp.int32, (blk, 2 * blk), 1)
    off = row + blk - col
    first_col = jnp.where(n == 0, blk, 0)
    valid = (off >= 0) & (off <= blk) & (col >= first_col)
    dist = jnp.where(valid, (off * dil).astype(F32), NEG_BIG)
    lane = lax.broadcasted_iota(jnp.int32, (blk, LANES), 1)
    res = []
    for pair in range(DIL_HEADS // 2):
        lo = pair * LANES
        qp = q[:, lo:lo + LANES]
        kp = k[:, lo:lo + LANES]
        vp = v[:, lo:lo + LANES]
        stats = []
        for hh in range(2):
            slope = 2.0 ** (-8.0 * (2 * pair + hh + 1) / DIL_HEADS)
            keep = (lane < HEAD_DIM) if hh == 0 else (lane >= HEAD_DIM)
            qh = jnp.where(keep, qp, jnp.zeros_like(qp))
            s = _dot_nt(qh, kp) - slope * dist
            m = jnp.max(s, axis=-1, keepdims=True)
            p = jnp.exp(s - m)
            l = jnp.sum(p, axis=-1, keepdims=True)
            stats.append((m, l, _dot(p.astype(BF16), vp)))
        res.append(stats)
    return res, lane


def _dil_branch_kernel(q_ref, kp_ref, kc_ref, vp_ref, vc_ref, o_ref, lse_ref, *, dil):
    n = pl.program_id(2)
    res, lane = _dil_scores(q_ref, kp_ref, kc_ref, vp_ref, vc_ref, n, dil)
    for pair, ((m0, l0, a0), (m1, l1, a1)) in enumerate(res):
        lo = pair * LANES
        first = lane < HEAD_DIM
        o_ref[0, :, lo:lo + LANES] = (jnp.where(first, a0, a1) / jnp.where(first, l0, l1)).astype(o_ref.dtype)
        lse_ref[0, 0, :, 2 * pair:2 * pair + 1] = m0 + jnp.log(l0)
        lse_ref[0, 0, :, 2 * pair + 1:2 * pair + 2] = m1 + jnp.log(l1)


def _dil_final_kernel(q_ref, kp_ref, kc_ref, vp_ref, vc_ref, oa_ref, la_ref, ob_ref, lb_ref, o_ref):
    n = pl.program_id(1)
    res, lane = _dil_scores(q_ref, kp_ref, kc_ref, vp_ref, vc_ref, n, 1)
    la = la_ref[0]
    lb = lb_ref[0]
    first = lane < HEAD_DIM
    for pair, ((m0, l0, a0), (m1, l1, a1)) in enumerate(res):
        lo = pair * LANES
        h0, h1 = 2 * pair, 2 * pair + 1
        lse_c = jnp.where(first, m0 + jnp.log(l0), m1 + jnp.log(l1))
        lse_a = jnp.where(first, la[:, h0:h0 + 1], la[:, h1:h1 + 1])
        lse_b = jnp.where(first, lb[:, h0:h0 + 1], lb[:, h1:h1 + 1])
        top = jnp.maximum(lse_c, jnp.maximum(lse_a, lse_b))
        wc = jnp.exp(lse_c - top)
        wa = jnp.exp(lse_a - top)
        wb = jnp.exp(lse_b - top)
        o_c = jnp.where(first, a0, a1) / jnp.where(first, l0, l1)
        o_a = oa_ref[0, :, lo:lo + LANES].astype(F32)
        o_b = ob_ref[0, :, lo:lo + LANES].astype(F32)
        mixed = (wc * o_c + wa * o_a + wb * o_b) / (wc + wa + wb)
        o_ref[0, :, lo:lo + LANES] = mixed.astype(o_ref.dtype)


def _dil_branch(qv, kv, vv, dil):
    B, M, _ = qv.shape
    blk = DIL_BLOCK
    cur = pl.BlockSpec((1, blk, DIL_WIDTH), lambda b, r, n: (b, n, r))
    prev = pl.BlockSpec((1, blk, DIL_WIDTH), lambda b, r, n: (b, jnp.maximum(n - 1, 0), r))
    return pl.pallas_call(
        functools.partial(_dil_branch_kernel, dil=dil),
        grid=(B, dil, M // blk),
        in_specs=[cur, prev, cur, prev, cur],
        out_specs=[cur, pl.BlockSpec((1, 1, blk, DIL_HEADS), lambda b, r, n: (b, r, n, 0))],
        out_shape=[jax.ShapeDtypeStruct(qv.shape, BF16),
                   jax.ShapeDtypeStruct((B, dil, M, DIL_HEADS), F32)],
        compiler_params=_params("parallel", "parallel", "parallel"),
        name=f"dilated_d{dil}",
    )(qv, kv, kv, vv, vv)


def _dil_final(q, k, v, o_a, lse_a, o_b, lse_b):
    B, S, _ = q.shape
    blk = DIL_BLOCK
    cur = pl.BlockSpec((1, blk, DIL_WIDTH), lambda b, n: (b, n, 0))
    prev = pl.BlockSpec((1, blk, DIL_WIDTH), lambda b, n: (b, jnp.maximum(n - 1, 0), 0))
    lse = pl.BlockSpec((1, blk, DIL_HEADS), lambda b, n: (b, n, 0))
    return pl.pallas_call(
        _dil_final_kernel,
        grid=(B, S // blk),
        in_specs=[cur, prev, cur, prev, cur, cur, lse, cur, lse],
        out_specs=cur,
        out_shape=jax.ShapeDtypeStruct(q.shape, BF16),
        compiler_params=_params("parallel", "parallel"),
        name="dilated_d1_mix",
    )(q, k, k, v, v, o_a, lse_a, o_b, lse_b)


def _dilated_attention(qd, kd, vd):
    B, S, W = qd.shape
    partial = []
    for window, dil in DIL_PAIRS[1:]:
        assert window // dil == DIL_BLOCK and S % (dil * DIL_BLOCK) == 0
        view = (B, S // dil, dil * W)
        o, lse = _dil_branch(qd.reshape(view), kd.reshape(view), vd.reshape(view), dil)
        lse = lse.transpose(0, 2, 1, 3).reshape(B, S, DIL_HEADS)
        partial.append((o.reshape(B, S, W), lse))
    assert DIL_PAIRS[0] == (DIL_BLOCK, 1)
    (o_a, lse_a), (o_b, lse_b) = partial
    return _dil_final(qd, kd, vd, o_a, lse_a, o_b, lse_b)


def _layer_norm(y, g, b):
    mu = jnp.mean(y, axis=-1, keepdims=True)
    d = y - mu
    var = jnp.mean(d * d, axis=-1, keepdims=True)
    return d * lax.rsqrt(var + LN_EPS) * g + b


def _out_proj_kernel(om_ref, od_ref, x_ref, wo_ref, g_ref, b_ref, o_ref):
    half = om_ref.shape[-1]
    mix = _dot(om_ref[0], wo_ref[:half, :]) + _dot(od_ref[0], wo_ref[half:, :])
    o_ref[0] = _layer_norm(DN_ALPHA * x_ref[0] + mix, g_ref[...], b_ref[...])


def _out_projection(o_mla, o_dil, x, w_o, g, b):
    B, S, _ = x.shape
    rows = min(PROJ_ROWS, S)
    half_spec = pl.BlockSpec((1, rows, o_mla.shape[-1]), lambda bi, i: (bi, i, 0))
    x_spec = pl.BlockSpec((1, rows, D_MODEL), lambda bi, i: (bi, i, 0))
    return pl.pallas_call(
        _out_proj_kernel,
        grid=(B, S // rows),
        in_specs=[half_spec, half_spec, x_spec, _const_spec(w_o.shape), _const_spec(g.shape), _const_spec(b.shape)],
        out_specs=x_spec,
        out_shape=jax.ShapeDtypeStruct(x.shape, F32),
        compiler_params=_params("parallel", "parallel"),
        name="out_projection_ln",
    )(o_mla, o_dil, x, w_o, g, b)


def _ffn_kernel(x_ref, halo_ref, wa_ref, wg_ref, cw_ref, wd_ref, g_ref, b_ref, o_ref, acc_ref):
    rows = x_ref.shape[1]
    pad = halo_ref.shape[1]
    x = x_ref[0]
    halo = jnp.where(pl.program_id(1) == 0, 0.0, halo_ref[0])
    xe = jnp.concatenate([halo, x], axis=0).astype(BF16)
    acc_ref[...] = jnp.zeros_like(acc_ref)

    def conv(u, taps):
        y = taps[3:4, :]
        for j in range(CONV_WIDTH):
            shift = CONV_WIDTH - 1 - j
            y = y + taps[j:j + 1, :] * u[pad - shift:pad - shift + rows, :]
        return y

    def body(c, carry):
        taps = cw_ref[c]
        ya = conv(_dot(xe, wa_ref[c]), taps[0:4])
        yg = conv(_dot(xe, wg_ref[c]), taps[4:8])
        hidden = (jax.nn.gelu(yg) * ya).astype(BF16)
        acc_ref[...] += _dot(hidden, wd_ref[c])
        return carry

    lax.fori_loop(0, wa_ref.shape[0], body, 0)
    o_ref[0] = _layer_norm(DN_ALPHA * x + acc_ref[...], g_ref[...], b_ref[...])


def _ffn(x1, wa, wg, cw, wd, g, b):
    B, S, _ = x1.shape
    rows = min(FFN_ROWS, S)
    pad = BF16_ROWS
    x_spec = pl.BlockSpec((1, rows, D_MODEL), lambda bi, i: (bi, i, 0))
    halo_spec = pl.BlockSpec((1, pad, D_MODEL), lambda bi, i: (bi, jnp.maximum(i * (rows // pad) - 1, 0), 0))
    return pl.pallas_call(
        _ffn_kernel,
        grid=(B, S // rows),
        in_specs=[x_spec, halo_spec, _const_spec(wa.shape), _const_spec(wg.shape), _const_spec(cw.shape),
                  _const_spec(wd.shape), _const_spec(g.shape), _const_spec(b.shape)],
        out_specs=x_spec,
        out_shape=jax.ShapeDtypeStruct(x1.shape, F32),
        scratch_shapes=[pltpu.VMEM((rows, D_MODEL), F32)],
        compiler_params=_params("parallel", "parallel"),
        name="conv_ffn_ln",
    )(x1, x1, wa, wg, cw, wd, g, b)


def _pad_cols(w, width):
    return jnp.pad(w, ((0, 0), (0, width - w.shape[1])))


def _head_groups(w, offset=None):
    rank, heads, e = w.shape
    out = jnp.zeros((rank, heads, LANES), w.dtype)
    for hd in range(heads):
        lo = 0 if offset is None else offset[hd]
        out = out.at[:, hd, lo:lo + e].set(w[:, hd, :])
    return out.reshape(rank, heads * LANES)


def _prepare(w_in, g_cq, g_ckv, w_uq, w_uk, w_uv, w_o, ln1_g, ln1_b, w_up, conv_w, conv_b, w_down, ln2_g, ln2_b):
    r0, r1, r2 = MLA_Q_RANK, MLA_Q_RANK + MLA_KV_RANK, MLA_Q_RANK + MLA_KV_RANK + MLA_ROPE_DIM
    w_in_p = jnp.concatenate([w_in[:, :r1], _pad_cols(w_in[:, r1:r2], LANES), w_in[:, r2:]], axis=1).astype(BF16)

    half = MLA_ROPE_DIM // 2
    rope = w_uq[:, :, MLA_NOPE_DIM:]
    swapped = jnp.concatenate([-rope[:, :, half:], rope[:, :, :half]], axis=-1)
    swapped = jnp.concatenate([jnp.zeros_like(w_uq[:, :, :MLA_NOPE_DIM]), swapped], axis=-1)
    wq2 = jnp.concatenate([_head_groups(w_uq), _head_groups(swapped)], axis=1).astype(BF16)

    v_off = [0 if hd % 2 == 0 else HEAD_DIM for hd in range(MLA_HEADS)]
    wkv = jnp.concatenate([_head_groups(w_uk), _head_groups(w_uv, v_off)], axis=1).astype(BF16)
    vone = jnp.zeros((1, MLA_HEADS, LANES), F32)
    for hd in range(MLA_HEADS):
        vone = vone.at[0, hd, HEAD_DIM if hd % 2 == 0 else 0].set(1.0)
    vone = vone.reshape(1, MLA_HEADS * LANES)

    n_chunks = D_FF // FFN_CHUNK
    wa = w_up[:, :D_FF].reshape(D_MODEL, n_chunks, FFN_CHUNK).transpose(1, 0, 2).astype(BF16)
    wg = w_up[:, D_FF:].reshape(D_MODEL, n_chunks, FFN_CHUNK).transpose(1, 0, 2).astype(BF16)
    taps = jnp.concatenate([conv_w, conv_b[None, :]], axis=0)
    cw = jnp.concatenate([taps[:, :D_FF].reshape(4, n_chunks, FFN_CHUNK),
                          taps[:, D_FF:].reshape(4, n_chunks, FFN_CHUNK)], axis=0).transpose(1, 0, 2)
    wd = w_down.reshape(n_chunks, FFN_CHUNK, D_MODEL).astype(BF16)
    row = lambda a: a.reshape(1, -1)
    return dict(w_in_p=w_in_p, gq=row(g_cq), gkv=row(g_ckv), wq2=wq2, wkv=wkv, vone=vone,
                w_o=w_o.astype(BF16), ln1=(row(ln1_g), row(ln1_b)), wa=wa, wg=wg, cw=cw, wd=wd,
                ln2=(row(ln2_g), row(ln2_b)))


def _freq_lanes():
    half = MLA_ROPE_DIM // 2
    freqs = ROPE_THETA ** (-jnp.arange(half, dtype=F32) / half)
    zeros = jnp.zeros((MLA_NOPE_DIM,), F32)
    return jnp.concatenate([zeros, freqs, freqs, jnp.zeros((LANES - MLA_NOPE_DIM - MLA_ROPE_DIM,), F32)])[None, :]


def kernel(x, w_in, g_cq, g_ckv, w_uq, w_uk, w_uv, w_o, ln1_g, ln1_b, w_up, conv_w, conv_b, w_down, ln2_g, ln2_b):
    B, S, _ = x.shape
    p = _prepare(w_in, g_cq, g_ckv, w_uq, w_uk, w_uv, w_o, ln1_g, ln1_b, w_up, conv_w, conv_b, w_down, ln2_g, ln2_b)
    scale = 1.0 / math.sqrt(MLA_NOPE_DIM + MLA_ROPE_DIM)
    tables = _rope_tables(S, _freq_lanes(), scale)
    q, k, v, qd, kd, vd = _projection(x, p["w_in_p"], p["gq"], p["gkv"], p["wq2"], p["wkv"], tables, p["vone"])
    o_mla = _mla_attention(q, k, v)
    o_dil = _dilated_attention(qd, kd, vd)
    x1 = _out_projection(o_mla, o_dil, x, p["w_o"], *p["ln1"])
    return _ffn(x1, p["wa"], p["wg"], p["cw"], p["wd"], *p["ln2"])
```

```python
import functools
import math

import jax
import jax.numpy as jnp
from jax import lax
from jax.experimental import pallas as pl
from jax.experimental.pallas import tpu as pltpu

D_MODEL = 1024
HEAD_DIM = 64
MLA_HEADS = 8
MLA_Q_RANK = 256
MLA_KV_RANK = 128
MLA_NOPE_DIM = 64
MLA_ROPE_DIM = 32
ROPE_THETA = 10000.0
DIL_HEADS = 8
DIL_PAIRS = ((128, 1), (512, 4), (2048, 16))
DIL_BLOCK = 128
DIL_WIDTH = DIL_HEADS * HEAD_DIM
D_FF = 2816
CONV_WIDTH = 3
DEPTH = 1
DN_ALPHA = (2.0 * DEPTH) ** 0.25
LN_EPS = 1e-5
RMS_EPS = 1e-6

LANES = 128
BF16_ROWS = 16
VMEM_LIMIT = 56 * 1024 * 1024

PROJ_ROWS = 512
MLA_BLOCK = 512
FFN_ROWS = 512
FFN_CHUNK = 256
NEG_BIG = 1e30

BF16 = jnp.bfloat16
F32 = jnp.float32


def _dot(a, b):
    return jnp.dot(a, b, preferred_element_type=F32)


def _dot_nt(a, b):
    return lax.dot_general(a, b, (((1,), (1,)), ((), ())), preferred_element_type=F32)


def _params(*sem):
    return pltpu.CompilerParams(dimension_semantics=sem, vmem_limit_bytes=VMEM_LIMIT)


def _const_spec(shape):
    zeros = (0,) * len(shape)
    return pl.BlockSpec(shape, lambda *_: zeros, pipeline_mode=pl.Buffered(1))


def _rope_table_kernel(freq_ref, out_ref, *, scale):
    rows = out_ref.shape[1]
    pos = lax.broadcasted_iota(jnp.int32, (rows, LANES), 0).astype(F32)
    ang = pos * freq_ref[...]
    c = jnp.cos(ang)
    s = jnp.sin(ang)
    out_ref[0] = c
    out_ref[1] = s
    out_ref[2] = c * scale
    out_ref[3] = s * scale


def _rope_tables(seq, freq_lanes, scale):
    return pl.pallas_call(
        functools.partial(_rope_table_kernel, scale=scale),
        out_shape=jax.ShapeDtypeStruct((4, seq, LANES), F32),
        name="rope_tables",
    )(freq_lanes)


def _rms(x, g):
    ms = jnp.mean(x * x, axis=-1, keepdims=True)
    return x * lax.rsqrt(ms + RMS_EPS) * g


def _proj_kernel(x_ref, w_in_ref, gq_ref, gkv_ref, wq_ref, wk_ref, wvt_ref, tab_ref, vone_ref,
                 q_ref, k_ref, vt_ref, qd_ref, kd_ref, vd_ref):
    xb = x_ref[0].astype(BF16)
    h = _dot(xb, w_in_ref[...])
    cq = _rms(h[:, :MLA_Q_RANK], gq_ref[...]).astype(BF16)
    ckv = _rms(h[:, MLA_Q_RANK:MLA_Q_RANK + MLA_KV_RANK], gkv_ref[...]).astype(BF16)
    kr = h[:, 3 * LANES:4 * LANES]
    cos, sin, cos_q, sin_q = tab_ref[0], tab_ref[1], tab_ref[2], tab_ref[3]

    k_plain = pltpu.roll(kr, 64, 1)
    k_swap = pltpu.roll(kr, 80, 1) - pltpu.roll(kr, 48, 1)
    k_rope = k_plain * cos + k_swap * sin

    q2 = _dot(cq, wq_ref[...])
    kn = _dot(ckv, wk_ref[...])
    hw = MLA_HEADS * LANES
    for hd in range(MLA_HEADS):
        lo = hd * LANES
        qh = q2[:, lo:lo + LANES] * cos_q + q2[:, hw + lo:hw + lo + LANES] * sin_q
        q_ref[0, hd] = qh.astype(BF16)
        k_ref[0, hd] = (kn[:, lo:lo + LANES] + k_rope).astype(BF16)
        vt_ref[0, hd, 0] = (_dot_nt(wvt_ref[hd], ckv) + vone_ref[...]).astype(BF16)

    base = 4 * LANES
    qd_ref[0] = (h[:, base:base + DIL_WIDTH] * (1.0 / math.sqrt(HEAD_DIM))).astype(BF16)
    kd_ref[0] = h[:, base + DIL_WIDTH:base + 2 * DIL_WIDTH].astype(BF16)
    vd_ref[0] = h[:, base + 2 * DIL_WIDTH:base + 3 * DIL_WIDTH].astype(BF16)


def _projection(x, w_in_p, gq, gkv, wq2, wk, wvt, tables, vone):
    B, S, _ = x.shape
    rows = min(PROJ_ROWS, S)
    head_shape = jax.ShapeDtypeStruct((B, MLA_HEADS, S, LANES), BF16)
    vt_shape = jax.ShapeDtypeStruct((B, MLA_HEADS, S // rows, LANES, rows), BF16)
    dil_shape = jax.ShapeDtypeStruct((B, S, DIL_WIDTH), BF16)
    head_spec = pl.BlockSpec((1, MLA_HEADS, rows, LANES), lambda b, i: (b, 0, i, 0))
    vt_spec = pl.BlockSpec((1, MLA_HEADS, 1, LANES, rows), lambda b, i: (b, 0, i, 0, 0))
    dil_spec = pl.BlockSpec((1, rows, DIL_WIDTH), lambda b, i: (b, i, 0))
    return pl.pallas_call(
        _proj_kernel,
        grid=(B, S // rows),
        in_specs=[
            pl.BlockSpec((1, rows, D_MODEL), lambda b, i: (b, i, 0)),
            _const_spec(w_in_p.shape),
            _const_spec(gq.shape),
            _const_spec(gkv.shape),
            _const_spec(wq2.shape),
            _const_spec(wk.shape),
            _const_spec(wvt.shape),
            pl.BlockSpec((4, rows, LANES), lambda b, i: (0, i, 0)),
            _const_spec(vone.shape),
        ],
        out_specs=[head_spec, head_spec, vt_spec, dil_spec, dil_spec, dil_spec],
        out_shape=[head_shape, head_shape, vt_shape, dil_shape, dil_shape, dil_shape],
        compiler_params=_params("parallel", "parallel"),
        name="in_projection",
    )(x, w_in_p, gq, gkv, wq2, wk, wvt, tables, vone)


def _mla_kernel(q_ref, k_ref, vt_ref, o_ref, acc_ref, m_ref, st_ref, *, blk):
    qi = pl.program_id(2)
    acc_ref[...] = jnp.zeros_like(acc_ref)
    m_ref[...] = jnp.full_like(m_ref, -NEG_BIG)

    def scores(hh, j):
        start = pl.multiple_of(j * blk, blk)
        return _dot_nt(k_ref[0, hh, pl.ds(start, blk), :], q_ref[0, hh])

    def update(hh, j, st, masked):
        if masked:
            key = lax.broadcasted_iota(jnp.int32, (blk, blk), 0)
            qry = lax.broadcasted_iota(jnp.int32, (blk, blk), 1)
            st = jnp.where(key <= qry, st, -NEG_BIG)
        m_old = m_ref[hh]
        m_new = jnp.maximum(m_old, jnp.max(st, axis=0, keepdims=True))
        pt = jnp.exp2(st - m_new).astype(BF16)
        acc_ref[hh] = acc_ref[hh] * jnp.exp2(m_old - m_new) + _dot(vt_ref[0, hh, j], pt)
        m_ref[hh] = m_new

    st_ref[...] = scores(0, 0)

    def body(j, carry):
        st_a = st_ref[...]
        st_b = scores(1, j)
        update(0, j, st_a, False)
        st_a_next = scores(0, j + 1)
        update(1, j, st_b, False)
        st_ref[...] = st_a_next
        return carry

    lax.fori_loop(0, qi, body, 0)
    st_a = st_ref[...]
    st_b = scores(1, qi)
    update(0, qi, st_a, True)
    update(1, qi, st_b, True)
    outs = []
    for hh in range(2):
        acc = acc_ref[hh]
        outs.append(acc[:HEAD_DIM] / acc[HEAD_DIM:HEAD_DIM + 1])
    o_ref[0] = jnp.concatenate(outs, axis=0).T.astype(o_ref.dtype)


def _mla_attention(q, k, vt):
    B, H, S, _ = q.shape
    blk = vt.shape[-1]
    k_spec = pl.BlockSpec((1, 2, S, LANES), lambda b, hp, i: (b, hp, 0, 0))
    vt_spec = pl.BlockSpec((1, 2, S // blk, LANES, blk), lambda b, hp, i: (b, hp, 0, 0, 0))
    return pl.pallas_call(
        functools.partial(_mla_kernel, blk=blk),
        grid=(B, H // 2, S // blk),
        in_specs=[pl.BlockSpec((1, 2, blk, LANES), lambda b, hp, i: (b, hp, i, 0)), k_spec, vt_spec],
        out_specs=pl.BlockSpec((1, blk, LANES), lambda b, hp, i: (b, i, hp)),
        out_shape=jax.ShapeDtypeStruct((B, S, H * HEAD_DIM), BF16),
        scratch_shapes=[pltpu.VMEM((2, LANES, blk), F32), pltpu.VMEM((2, 1, blk), F32),
                        pltpu.VMEM((blk, blk), F32)],
        compiler_params=_params("parallel", "parallel", "arbitrary"),
        name="mla_attention",
    )(q, k, vt)


def _dil_scores(q_ref, kp_ref, kc_ref, vp_ref, vc_ref, n, dil):
    blk = DIL_BLOCK
    q = q_ref[0]
    k = jnp.concatenate([kp_ref[0], kc_ref[0]], axis=0)
    v = jnp.concatenate([vp_ref[0], vc_ref[0]], axis=0)
    row = lax.broadcasted_iota(jnp.int32, (blk, 2 * blk), 0)
    col = lax.broadcasted_iota(jnp.int32, (blk, 2 * blk), 1)
    off = row + blk - col
    first_col = jnp.where(n == 0, blk, 0)
    valid = (off >= 0) & (off <= blk) & (col >= first_col)
    dist = jnp.where(valid, (off * dil).astype(F32), NEG_BIG)
    lane = lax.broadcasted_iota(jnp.int32, (blk, LANES), 1)
    res = []
    for pair in range(DIL_HEADS // 2):
        lo = pair * LANES
        qp = q[:, lo:lo + LANES]
        kp = k[:, lo:lo + LANES]
        vp = v[:, lo:lo + LANES]
        stats = []
        for hh in range(2):
            slope = 2.0 ** (-8.0 * (2 * pair + hh + 1) / DIL_HEADS)
            keep = (lane < HEAD_DIM) if hh == 0 else (lane >= HEAD_DIM)
            qh = jnp.where(keep, qp, jnp.zeros_like(qp))
            s = _dot_nt(qh, kp) - slope * dist
            m = jnp.max(s, axis=-1, keepdims=True)
            p = jnp.exp(s - m)
            l = jnp.sum(p, axis=-1, keepdims=True)
            stats.append((m, l, _dot(p.astype(BF16), vp)))
        res.append(stats)
    return res, lane


def _dil_branch_kernel(q_ref, kp_ref, kc_ref, vp_ref, vc_ref, o_ref, lse_ref, *, dil):
    n = pl.program_id(2)
    res, lane = _dil_scores(q_ref, kp_ref, kc_ref, vp_ref, vc_ref, n, dil)
    for pair, ((m0, l0, a0), (m1, l1, a1)) in enumerate(res):
        lo = pair * LANES
        first = lane < HEAD_DIM
        o_ref[0, :, lo:lo + LANES] = (jnp.where(first, a0, a1) / jnp.where(first, l0, l1)).astype(o_ref.dtype)
        lse_ref[0, 0, :, 2 * pair:2 * pair + 1] = m0 + jnp.log(l0)
        lse_ref[0, 0, :, 2 * pair + 1:2 * pair + 2] = m1 + jnp.log(l1)


def _dil_final_kernel(q_ref, kp_ref, kc_ref, vp_ref, vc_ref, oa_ref, la_ref, ob_ref, lb_ref, o_ref):
    n = pl.program_id(1)
    res, lane = _dil_scores(q_ref, kp_ref, kc_ref, vp_ref, vc_ref, n, 1)
    la = la_ref[0]
    lb = lb_ref[0]
    first = lane < HEAD_DIM
    for pair, ((m0, l0, a0), (m1, l1, a1)) in enumerate(res):
        lo = pair * LANES
        h0, h1 = 2 * pair, 2 * pair + 1
        lse_c = jnp.where(first, m0 + jnp.log(l0), m1 + jnp.log(l1))
        lse_a = jnp.where(first, la[:, h0:h0 + 1], la[:, h1:h1 + 1])
        lse_b = jnp.where(first, lb[:, h0:h0 + 1], lb[:, h1:h1 + 1])
        top = jnp.maximum(lse_c, jnp.maximum(lse_a, lse_b))
        wc = jnp.exp(lse_c - top)
        wa = jnp.exp(lse_a - top)
        wb = jnp.exp(lse_b - top)
        o_c = jnp.where(first, a0, a1) / jnp.where(first, l0, l1)
        o_a = oa_ref[0, :, lo:lo + LANES].astype(F32)
        o_b = ob_ref[0, :, lo:lo + LANES].astype(F32)
        mixed = (wc * o_c + wa * o_a + wb * o_b) / (wc + wa + wb)
        o_ref[0, :, lo:lo + LANES] = mixed.astype(o_ref.dtype)


def _dil_branch(qv, kv, vv, dil):
    B, M, _ = qv.shape
    blk = DIL_BLOCK
    cur = pl.BlockSpec((1, blk, DIL_WIDTH), lambda b, r, n: (b, n, r))
    prev = pl.BlockSpec((1, blk, DIL_WIDTH), lambda b, r, n: (b, jnp.maximum(n - 1, 0), r))
    return pl.pallas_call(
        functools.partial(_dil_branch_kernel, dil=dil),
        grid=(B, dil, M // blk),
        in_specs=[cur, prev, cur, prev, cur],
        out_specs=[cur, pl.BlockSpec((1, 1, blk, DIL_HEADS), lambda b, r, n: (b, r, n, 0))],
        out_shape=[jax.ShapeDtypeStruct(qv.shape, BF16),
                   jax.ShapeDtypeStruct((B, dil, M, DIL_HEADS), F32)],
        compiler_params=_params("parallel", "parallel", "parallel"),
        name=f"dilated_d{dil}",
    )(qv, kv, kv, vv, vv)


def _dil_final(q, k, v, o_a, lse_a, o_b, lse_b):
    B, S, _ = q.shape
    blk = DIL_BLOCK
    cur = pl.BlockSpec((1, blk, DIL_WIDTH), lambda b, n: (b, n, 0))
    prev = pl.BlockSpec((1, blk, DIL_WIDTH), lambda b, n: (b, jnp.maximum(n - 1, 0), 0))
    lse = pl.BlockSpec((1, blk, DIL_HEADS), lambda b, n: (b, n, 0))
    return pl.pallas_call(
        _dil_final_kernel,
        grid=(B, S // blk),
        in_specs=[cur, prev, cur, prev, cur, cur, lse, cur, lse],
        out_specs=cur,
        out_shape=jax.ShapeDtypeStruct(q.shape, BF16),
        compiler_params=_params("parallel", "parallel"),
        name="dilated_d1_mix",
    )(q, k, k, v, v, o_a, lse_a, o_b, lse_b)


def _dilated_attention(qd, kd, vd):
    B, S, W = qd.shape
    partial = []
    for window, dil in DIL_PAIRS[1:]:
        assert window // dil == DIL_BLOCK and S % (dil * DIL_BLOCK) == 0
        view = (B, S // dil, dil * W)
        o, lse = _dil_branch(qd.reshape(view), kd.reshape(view), vd.reshape(view), dil)
        lse = lse.transpose(0, 2, 1, 3).reshape(B, S, DIL_HEADS)
        partial.append((o.reshape(B, S, W), lse))
    assert DIL_PAIRS[0] == (DIL_BLOCK, 1)
    (o_a, lse_a), (o_b, lse_b) = partial
    return _dil_final(qd, kd, vd, o_a, lse_a, o_b, lse_b)


def _layer_norm(y, g, b):
    mu = jnp.mean(y, axis=-1, keepdims=True)
    d = y - mu
    var = jnp.mean(d * d, axis=-1, keepdims=True)
    return d * lax.rsqrt(var + LN_EPS) * g + b


def _out_proj_kernel(om_ref, od_ref, x_ref, wo_ref, g_ref, b_ref, o_ref):
    half = om_ref.shape[-1]
    mix = _dot(om_ref[0], wo_ref[:half, :]) + _dot(od_ref[0], wo_ref[half:, :])
    o_ref[0] = _layer_norm(DN_ALPHA * x_ref[0] + mix, g_ref[...], b_ref[...])


def _out_projection(o_mla, o_dil, x, w_o, g, b):
    B, S, _ = x.shape
    rows = min(PROJ_ROWS, S)
    half_spec = pl.BlockSpec((1, rows, o_mla.shape[-1]), lambda bi, i: (bi, i, 0))
    x_spec = pl.BlockSpec((1, rows, D_MODEL), lambda bi, i: (bi, i, 0))
    return pl.pallas_call(
        _out_proj_kernel,
        grid=(B, S // rows),
        in_specs=[half_spec, half_spec, x_spec, _const_spec(w_o.shape), _const_spec(g.shape), _const_spec(b.shape)],
        out_specs=x_spec,
        out_shape=jax.ShapeDtypeStruct(x.shape, F32),
        compiler_params=_params("parallel", "parallel"),
        name="out_projection_ln",
    )(o_mla, o_dil, x, w_o, g, b)


def _ffn_kernel(x_ref, halo_ref, wa_ref, wg_ref, cw_ref, wd_ref, g_ref, b_ref, o_ref, acc_ref):
    rows = x_ref.shape[1]
    pad = halo_ref.shape[1]
    x = x_ref[0]
    halo = jnp.where(pl.program_id(1) == 0, 0.0, halo_ref[0])
    xe = jnp.concatenate([halo, x], axis=0).astype(BF16)
    acc_ref[...] = jnp.zeros_like(acc_ref)

    def conv(u, taps):
        y = taps[3:4, :]
        for j in range(CONV_WIDTH):
            shift = CONV_WIDTH - 1 - j
            y = y + taps[j:j + 1, :] * u[pad - shift:pad - shift + rows, :]
        return y

    def body(c, carry):
        taps = cw_ref[c]
        ya = conv(_dot(xe, wa_ref[c]), taps[0:4])
        yg = conv(_dot(xe, wg_ref[c]), taps[4:8])
        hidden = (jax.nn.gelu(yg) * ya).astype(BF16)
        acc_ref[...] += _dot(hidden, wd_ref[c])
        return carry

    lax.fori_loop(0, wa_ref.shape[0], body, 0)
    o_ref[0] = _layer_norm(DN_ALPHA * x + acc_ref[...], g_ref[...], b_ref[...])


def _ffn(x1, wa, wg, cw, wd, g, b):
    B, S, _ = x1.shape
    rows = min(FFN_ROWS, S)
    pad = BF16_ROWS
    x_spec = pl.BlockSpec((1, rows, D_MODEL), lambda bi, i: (bi, i, 0))
    halo_spec = pl.BlockSpec((1, pad, D_MODEL), lambda bi, i: (bi, jnp.maximum(i * (rows // pad) - 1, 0), 0))
    return pl.pallas_call(
        _ffn_kernel,
        grid=(B, S // rows),
        in_specs=[x_spec, halo_spec, _const_spec(wa.shape), _const_spec(wg.shape), _const_spec(cw.shape),
                  _const_spec(wd.shape), _const_spec(g.shape), _const_spec(b.shape)],
        out_specs=x_spec,
        out_shape=jax.ShapeDtypeStruct(x1.shape, F32),
        scratch_shapes=[pltpu.VMEM((rows, D_MODEL), F32)],
        compiler_params=_params("parallel", "parallel"),
        name="conv_ffn_ln",
    )(x1, x1, wa, wg, cw, wd, g, b)


def _pad_cols(w, width):
    return jnp.pad(w, ((0, 0), (0, width - w.shape[1])))


def _head_groups(w, offset=None):
    rank, heads, e = w.shape
    out = jnp.zeros((rank, heads, LANES), w.dtype)
    for hd in range(heads):
        lo = 0 if offset is None else offset[hd]
        out = out.at[:, hd, lo:lo + e].set(w[:, hd, :])
    return out.reshape(rank, heads * LANES)


def _prepare(w_in, g_cq, g_ckv, w_uq, w_uk, w_uv, w_o, ln1_g, ln1_b, w_up, conv_w, conv_b, w_down, ln2_g, ln2_b):
    r0, r1, r2 = MLA_Q_RANK, MLA_Q_RANK + MLA_KV_RANK, MLA_Q_RANK + MLA_KV_RANK + MLA_ROPE_DIM
    w_in_p = jnp.concatenate([w_in[:, :r1], _pad_cols(w_in[:, r1:r2], LANES), w_in[:, r2:]], axis=1).astype(BF16)

    half = MLA_ROPE_DIM // 2
    rope = w_uq[:, :, MLA_NOPE_DIM:]
    swapped = jnp.concatenate([-rope[:, :, half:], rope[:, :, :half]], axis=-1)
    swapped = jnp.concatenate([jnp.zeros_like(w_uq[:, :, :MLA_NOPE_DIM]), swapped], axis=-1)
    wq2 = jnp.concatenate([_head_groups(w_uq), _head_groups(swapped)], axis=1).astype(BF16)

    wk = _head_groups(w_uk).astype(BF16)
    wvt = jnp.pad(w_uv.transpose(1, 2, 0), ((0, 0), (0, LANES - HEAD_DIM), (0, 0))).astype(BF16)
    vone = jnp.zeros((LANES, 1), F32).at[HEAD_DIM, 0].set(1.0)

    n_chunks = D_FF // FFN_CHUNK
    wa = w_up[:, :D_FF].reshape(D_MODEL, n_chunks, FFN_CHUNK).transpose(1, 0, 2).astype(BF16)
    wg = w_up[:, D_FF:].reshape(D_MODEL, n_chunks, FFN_CHUNK).transpose(1, 0, 2).astype(BF16)
    taps = jnp.concatenate([conv_w, conv_b[None, :]], axis=0)
    cw = jnp.concatenate([taps[:, :D_FF].reshape(4, n_chunks, FFN_CHUNK),
                          taps[:, D_FF:].reshape(4, n_chunks, FFN_CHUNK)], axis=0).transpose(1, 0, 2)
    wd = w_down.reshape(n_chunks, FFN_CHUNK, D_MODEL).astype(BF16)
    row = lambda a: a.reshape(1, -1)
    return dict(w_in_p=w_in_p, gq=row(g_cq), gkv=row(g_ckv), wq2=wq2, wk=wk, wvt=wvt, vone=vone,
                w_o=w_o.astype(BF16), ln1=(row(ln1_g), row(ln1_b)), wa=wa, wg=wg, cw=cw, wd=wd,
                ln2=(row(ln2_g), row(ln2_b)))


def _freq_lanes():
    half = MLA_ROPE_DIM // 2
    freqs = ROPE_THETA ** (-jnp.arange(half, dtype=F32) / half)
    zeros = jnp.zeros((MLA_NOPE_DIM,), F32)
    return jnp.concatenate([zeros, freqs, freqs, jnp.zeros((LANES - MLA_NOPE_DIM - MLA_ROPE_DIM,), F32)])[None, :]


def kernel(x, w_in, g_cq, g_ckv, w_uq, w_uk, w_uv, w_o, ln1_g, ln1_b, w_up, conv_w, conv_b, w_down, ln2_g, ln2_b):
    B, S, _ = x.shape
    p = _prepare(w_in, g_cq, g_ckv, w_uq, w_uk, w_uv, w_o, ln1_g, ln1_b, w_up, conv_w, conv_b, w_down, ln2_g, ln2_b)
    scale = math.log2(math.e) / math.sqrt(MLA_NOPE_DIM + MLA_ROPE_DIM)
    tables = _rope_tables(S, _freq_lanes(), scale)
    q, k, vt, qd, kd, vd = _projection(x, p["w_in_p"], p["gq"], p["gkv"], p["wq2"], p["wk"], p["wvt"], tables,
                                       p["vone"])
    o_mla = _mla_attention(q, k, vt)
    o_dil = _dilated_attention(qd, kd, vd)
    x1 = _out_projection(o_mla, o_dil, x, p["w_o"], *p["ln1"])
    return _ffn(x1, p["wa"], p["wg"], p["cw"], p["wd"], *p["ln2"])
```

```python
import functools
import math

import jax
import jax.numpy as jnp
from jax import lax
from jax.experimental import pallas as pl
from jax.experimental.pallas import tpu as pltpu

D_MODEL = 1024
HEAD_DIM = 64
MLA_HEADS = 8
MLA_Q_RANK = 256
MLA_KV_RANK = 128
MLA_NOPE_DIM = 64
MLA_ROPE_DIM = 32
ROPE_THETA = 10000.0
DIL_HEADS = 8
DIL_PAIRS = ((128, 1), (512, 4), (2048, 16))
DIL_BLOCK = 128
DIL_WIDTH = DIL_HEADS * HEAD_DIM
D_FF = 2816
CONV_WIDTH = 3
DEPTH = 1
DN_ALPHA = (2.0 * DEPTH) ** 0.25
LN_EPS = 1e-5
RMS_EPS = 1e-6

LANES = 128
BF16_ROWS = 16
VMEM_LIMIT = 56 * 1024 * 1024

PROJ_ROWS = 512
DIL_SPAN = 2048
FFN_ROWS = 512
FFN_CHUNK = 256
NEG_BIG = 1e30

BF16 = jnp.bfloat16
F32 = jnp.float32


def _dot(a, b):
    return jnp.dot(a, b, preferred_element_type=F32)


def _dot_nt(a, b):
    return lax.dot_general(a, b, (((1,), (1,)), ((), ())), preferred_element_type=F32)


def _params(*sem, flags=None):
    return pltpu.CompilerParams(dimension_semantics=sem, vmem_limit_bytes=VMEM_LIMIT, flags=flags)


def _const_spec(shape):
    zeros = (0,) * len(shape)
    return pl.BlockSpec(shape, lambda *_: zeros, pipeline_mode=pl.Buffered(1))


def _rope_table_kernel(freq_ref, out_ref, *, scale):
    rows = out_ref.shape[1]
    pos = lax.broadcasted_iota(jnp.int32, (rows, LANES), 0).astype(F32)
    ang = pos * freq_ref[...]
    c = jnp.cos(ang)
    s = jnp.sin(ang)
    out_ref[0] = c
    out_ref[1] = s
    out_ref[2] = c * scale
    out_ref[3] = s * scale


def _rope_tables(seq, freq_lanes, scale):
    return pl.pallas_call(
        functools.partial(_rope_table_kernel, scale=scale),
        out_shape=jax.ShapeDtypeStruct((4, seq, LANES), F32),
        name="rope_tables",
    )(freq_lanes)


def _rms(x, g):
    ms = jnp.mean(x * x, axis=-1, keepdims=True)
    return x * lax.rsqrt(ms + RMS_EPS) * g


def _proj_kernel(x_ref, w_in_ref, gq_ref, gkv_ref, wq_ref, wk_ref, wvt_ref, tab_ref, vone_ref,
                 q_ref, k_ref, vt_ref, *rest):
    dil_refs, hd_ref = rest[:-1], rest[-1]
    rows = x_ref.shape[1]
    xb = x_ref[0].astype(BF16)
    h = _dot(xb, w_in_ref[...])
    cq = _rms(h[:, :MLA_Q_RANK], gq_ref[...]).astype(BF16)
    ckv = _rms(h[:, MLA_Q_RANK:MLA_Q_RANK + MLA_KV_RANK], gkv_ref[...]).astype(BF16)
    kr = h[:, 3 * LANES:4 * LANES]
    cos, sin, cos_q, sin_q = tab_ref[0], tab_ref[1], tab_ref[2], tab_ref[3]

    k_plain = pltpu.roll(kr, 64, 1)
    k_swap = pltpu.roll(kr, 80, 1) - pltpu.roll(kr, 48, 1)
    k_rope = k_plain * cos + k_swap * sin

    q2 = _dot(cq, wq_ref[...])
    kn = _dot(ckv, wk_ref[...])
    hw = MLA_HEADS * LANES
    for hd in range(MLA_HEADS):
        lo = hd * LANES
        qh = q2[:, lo:lo + LANES] * cos_q + q2[:, hw + lo:hw + lo + LANES] * sin_q
        q_ref[0, hd] = qh.astype(BF16)
        k_ref[0, hd] = (kn[:, lo:lo + LANES] + k_rope).astype(BF16)
        vt_ref[0, hd, 0] = (_dot_nt(wvt_ref[hd], ckv) + vone_ref[...]).astype(BF16)

    base = 4 * LANES
    groups = DIL_WIDTH // LANES
    for c in range(3 * groups):
        slab = h[:, base + c * LANES:base + (c + 1) * LANES]
        hd_ref[c] = slab * (math.log2(math.e) / math.sqrt(HEAD_DIM)) if c < groups else slab
    for di, (_, dil) in enumerate(DIL_PAIRS):
        for part in range(3):
            out = dil_refs[3 * di + part]
            for r in range(dil):
                for g in range(groups):
                    piece = hd_ref[part * groups + g, pl.ds(r, rows // dil, stride=dil), :]
                    out[0, r, :, g * LANES:(g + 1) * LANES] = piece.astype(BF16)


def _projection(x, w_in_p, gq, gkv, wq2, wk, wvt, tables, vone):
    B, S, _ = x.shape
    rows = min(PROJ_ROWS, S)
    head_shape = jax.ShapeDtypeStruct((B, MLA_HEADS, S, LANES), BF16)
    vt_shape = jax.ShapeDtypeStruct((B, MLA_HEADS, S // rows, LANES, rows), BF16)
    head_spec = pl.BlockSpec((1, MLA_HEADS, rows, LANES), lambda b, i: (b, 0, i, 0))
    vt_spec = pl.BlockSpec((1, MLA_HEADS, 1, LANES, rows), lambda b, i: (b, 0, i, 0, 0))
    dil_shapes, dil_specs = [], []
    for _, dil in DIL_PAIRS:
        assert rows % (dil * BF16_ROWS) == 0
        dil_shapes += [jax.ShapeDtypeStruct((B, dil, S // dil, DIL_WIDTH), BF16)] * 3
        dil_specs += [pl.BlockSpec((1, dil, rows // dil, DIL_WIDTH), lambda b, i: (b, 0, i, 0))] * 3
    outs = pl.pallas_call(
        _proj_kernel,
        grid=(B, S // rows),
        in_specs=[
            pl.BlockSpec((1, rows, D_MODEL), lambda b, i: (b, i, 0)),
            _const_spec(w_in_p.shape),
            _const_spec(gq.shape),
            _const_spec(gkv.shape),
            _const_spec(wq2.shape),
            _const_spec(wk.shape),
            _const_spec(wvt.shape),
            pl.BlockSpec((4, rows, LANES), lambda b, i: (0, i, 0)),
            _const_spec(vone.shape),
        ],
        out_specs=[head_spec, head_spec, vt_spec] + dil_specs,
        out_shape=[head_shape, head_shape, vt_shape] + dil_shapes,
        scratch_shapes=[pltpu.VMEM((3 * DIL_WIDTH // LANES, rows, LANES), F32)],
        compiler_params=_params("parallel", "parallel"),
        name="in_projection",
    )(x, w_in_p, gq, gkv, wq2, wk, wvt, tables, vone)
    return outs[0], outs[1], outs[2], [outs[3 + 3 * d:6 + 3 * d] for d in range(len(DIL_PAIRS))]


def _mla_kernel(q_ref, k_ref, vt_ref, o_ref, acc_ref, m_ref, st_ref, *, blk):
    qi = pl.program_id(2)
    acc_ref[...] = jnp.zeros_like(acc_ref)
    m_ref[...] = jnp.full_like(m_ref, -NEG_BIG)

    def scores(hh, j):
        start = pl.multiple_of(j * blk, blk)
        return _dot_nt(k_ref[0, hh, pl.ds(start, blk), :], q_ref[0, hh])

    def update(hh, j, st, masked):
        if masked:
            key = lax.broadcasted_iota(jnp.int32, (blk, blk), 0)
            qry = lax.broadcasted_iota(jnp.int32, (blk, blk), 1)
            st = jnp.where(key <= qry, st, -NEG_BIG)
        m_old = m_ref[hh]
        m_new = jnp.maximum(m_old, jnp.max(st, axis=0, keepdims=True))
        pt = jnp.exp2(st - m_new).astype(BF16)
        acc_ref[hh] = acc_ref[hh] * jnp.exp2(m_old - m_new) + _dot(vt_ref[0, hh, j], pt)
        m_ref[hh] = m_new

    st_ref[...] = scores(0, 0)

    def body(j, carry):
        st_a = st_ref[...]
        st_b = scores(1, j)
        update(0, j, st_a, False)
        st_a_next = scores(0, j + 1)
        update(1, j, st_b, False)
        st_ref[...] = st_a_next
        return carry

    lax.fori_loop(0, qi, body, 0)
    st_a = st_ref[...]
    st_b = scores(1, qi)
    update(0, qi, st_a, True)
    update(1, qi, st_b, True)
    outs = []
    for hh in range(2):
        acc = acc_ref[hh]
        outs.append(acc[:HEAD_DIM] / acc[HEAD_DIM:HEAD_DIM + 1])
    o_ref[0] = jnp.concatenate(outs, axis=0).T.astype(o_ref.dtype)


def _mla_attention(q, k, vt):
    B, H, S, _ = q.shape
    blk = vt.shape[-1]
    k_spec = pl.BlockSpec((1, 2, S, LANES), lambda b, hp, i: (b, hp, 0, 0))
    vt_spec = pl.BlockSpec((1, 2, S // blk, LANES, blk), lambda b, hp, i: (b, hp, 0, 0, 0))
    return pl.pallas_call(
        functools.partial(_mla_kernel, blk=blk),
        grid=(B, H // 2, S // blk),
        in_specs=[pl.BlockSpec((1, 2, blk, LANES), lambda b, hp, i: (b, hp, i, 0)), k_spec, vt_spec],
        out_specs=pl.BlockSpec((1, blk, LANES), lambda b, hp, i: (b, i, hp)),
        out_shape=jax.ShapeDtypeStruct((B, S, H * HEAD_DIM), BF16),
        scratch_shapes=[pltpu.VMEM((2, LANES, blk), F32), pltpu.VMEM((2, 1, blk), F32),
                        pltpu.VMEM((blk, blk), F32)],
        compiler_params=_params("parallel", "parallel", "arbitrary"),
        name="mla_attention",
    )(q, k, vt)


def _dil_bias_tiles(bias_ref, dil):
    blk = DIL_BLOCK
    key = lax.broadcasted_iota(jnp.int32, (2 * blk, blk), 0)
    qry = lax.broadcasted_iota(jnp.int32, (2 * blk, blk), 1)
    off = qry + blk - key
    steps = jnp.where((off >= 0) & (off <= blk), off.astype(F32), NEG_BIG)
    for pair in range(DIL_HEADS // 2):
        halves = []
        for hh in range(2):
            slope = 2.0 ** (-8.0 * (2 * pair + hh + 1) / DIL_HEADS)
            halves.append(steps * (-slope * dil * math.log2(math.e)))
        bias_ref[pair] = jnp.concatenate(halves, axis=1)


def _dil_kernel(q_ref, kc_ref, kp_ref, vc_ref, vp_ref, o_ref, lse_ref, k_buf, v_buf, st_scr, o_scr, bias_ref, *, dil):
    blk = DIL_BLOCK
    pairs = DIL_HEADS // 2
    nb = q_ref.shape[2] // blk
    units = dil * nb
    _dil_bias_tiles(bias_ref, dil)
    k_buf[:, :blk, :] = kp_ref[0]
    k_buf[:, blk:, :] = kc_ref[0]
    v_buf[:, :blk, :] = vp_ref[0]
    v_buf[:, blk:, :] = vc_ref[0]
    first_span = pl.program_id(1) == 0
    lane = lax.broadcasted_iota(jnp.int32, (blk, LANES), 1)
    first = lane < HEAD_DIM

    def split(u):
        if isinstance(u, int) or nb == 1:
            return u // nb, u % nb if nb > 1 else 0
        return lax.shift_right_logical(u, nb.bit_length() - 1), u & (nb - 1)

    def row0(i):
        return i * blk if isinstance(i, int) else pl.multiple_of(i * blk, blk)

    def scores(u):
        r, i = split(u)
        q = q_ref[0, r, pl.ds(row0(i), blk), :]
        keys = pl.ds(row0(i), 2 * blk)
        tiles = []
        for pair in range(pairs):
            cols = slice(pair * LANES, (pair + 1) * LANES)
            qp = q[:, cols]
            zero = jnp.zeros_like(qp)
            q2 = jnp.concatenate([jnp.where(first, qp, zero), jnp.where(first, zero, qp)], axis=0)
            tiles.append(_dot_nt(k_buf[r, keys, cols], q2))
        return tiles

    def finish(u, tiles):
        r, i = split(u)
        keys = pl.ds(row0(i), 2 * blk)
        pen = jnp.where(jnp.logical_and(first_span, i == 0), -NEG_BIG, 0.0)
        if dil == 1:
            rows = pl.ds(row0(i), blk)
        else:
            rows = pl.ds(r + i * (blk * dil), blk, stride=dil)
        for pair in range(pairs):
            cols = slice(pair * LANES, (pair + 1) * LANES)
            st = tiles[pair] + bias_ref[pair]
            st = jnp.concatenate([st[:blk] + pen, st[blk:]], axis=0)
            m = jnp.max(st, axis=0, keepdims=True)
            pt = jnp.exp2(st - m)
            l = jnp.sum(pt, axis=0, keepdims=True)
            res = lax.dot_general(v_buf[r, keys, cols], pt.astype(BF16), (((0,), (0,)), ((), ())),
                                  preferred_element_type=F32)
            inv = 1.0 / l
            ot = jnp.concatenate([res[:HEAD_DIM, :LANES] * inv[:, :LANES],
                                  res[HEAD_DIM:, LANES:] * inv[:, LANES:]], axis=0)
            o_scr[pair, rows, :] = ot.T
            lse2 = m + jnp.log2(l)
            lse_ref[0, r, i, 2 * pair:2 * pair + 1, :] = lse2[:, :LANES]
            lse_ref[0, r, i, 2 * pair + 1:2 * pair + 2, :] = lse2[:, LANES:]

    for pair, tile in enumerate(scores(0)):
        st_scr[pair] = tile

    def body(u, carry):
        tiles = [st_scr[pair] for pair in range(pairs)]
        ahead = scores(u + 1)
        finish(u, tiles)
        for pair, tile in enumerate(ahead):
            st_scr[pair] = tile
        return carry

    lax.fori_loop(0, units - 1, body, 0)
    finish(units - 1, [st_scr[pair] for pair in range(pairs)])
    for pair in range(pairs):
        o_ref[0, :, pair * LANES:(pair + 1) * LANES] = o_scr[pair].astype(o_ref.dtype)


def _dil_branch(q, k, v, dil):
    B, _, M, W = q.shape
    S = M * dil
    span = min(DIL_SPAN, S)
    assert span % (dil * DIL_BLOCK) == 0 and S % span == 0
    rows = span // dil
    nb = rows // DIL_BLOCK
    cur = pl.BlockSpec((1, dil, rows, W), lambda b, c: (b, 0, c, 0))
    prev = pl.BlockSpec((1, dil, DIL_BLOCK, W), lambda b, c: (b, 0, jnp.maximum(c * nb - 1, 0), 0))
    o, lse = pl.pallas_call(
        functools.partial(_dil_kernel, dil=dil),
        grid=(B, S // span),
        in_specs=[cur, cur, prev, cur, prev],
        out_specs=[pl.BlockSpec((1, span, W), lambda b, c: (b, c, 0)),
                   pl.BlockSpec((1, dil, nb, DIL_HEADS, LANES), lambda b, c: (b, 0, c, 0, 0))],
        out_shape=[jax.ShapeDtypeStruct((B, S, W), BF16),
                   jax.ShapeDtypeStruct((B, dil, M // DIL_BLOCK, DIL_HEADS, LANES), F32)],
        scratch_shapes=[pltpu.VMEM((dil, DIL_BLOCK + rows, W), BF16),
                        pltpu.VMEM((dil, DIL_BLOCK + rows, W), BF16),
                        pltpu.VMEM((DIL_HEADS // 2, 2 * DIL_BLOCK, 2 * LANES), F32),
                        pltpu.VMEM((DIL_HEADS // 2, span, LANES), F32),
                        pltpu.VMEM((DIL_HEADS // 2, 2 * DIL_BLOCK, 2 * LANES), F32)],
        compiler_params=_params("parallel", "arbitrary"),
        name=f"dilated_d{dil}",
    )(q, k, k, v, v)
    return o, lse.transpose(0, 2, 4, 1, 3).reshape(B, S, DIL_HEADS)


def _dilated_attention(qkv_per_dil):
    outs, lses = [], []
    for (window, dil), (q, k, v) in zip(DIL_PAIRS, qkv_per_dil):
        assert window // dil == DIL_BLOCK
        o, lse = _dil_branch(q, k, v, dil)
        outs.append(o)
        lses.append(lse)
    lse_all = jnp.concatenate(lses, axis=-1)
    return outs, jnp.pad(lse_all, ((0, 0), (0, 0), (0, LANES - lse_all.shape[-1])))


def _layer_norm(y, g, b):
    mu = jnp.mean(y, axis=-1, keepdims=True)
    d = y - mu
    var = jnp.mean(d * d, axis=-1, keepdims=True)
    return d * lax.rsqrt(var + LN_EPS) * g + b


def _out_proj_kernel(om_ref, o0_ref, o1_ref, o2_ref, lse_ref, x_ref, wo_ref, ex_ref, g_ref, b_ref, o_ref):
    H, W = DIL_HEADS, DIL_WIDTH
    l0 = lse_ref[0]
    l1 = pltpu.roll(l0, LANES - H, 1)
    l2 = pltpu.roll(l0, LANES - 2 * H, 1)
    top = jnp.maximum(l0, jnp.maximum(l1, l2))
    e0, e1, e2 = jnp.exp2(l0 - top), jnp.exp2(l1 - top), jnp.exp2(l2 - top)
    inv = 1.0 / (e0 + e1 + e2)
    lane = lax.broadcasted_iota(jnp.int32, l0.shape, 1)
    w = jnp.where(lane < H, e0 * inv,
                  jnp.where(lane < 2 * H, pltpu.roll(e1 * inv, H, 1), pltpu.roll(e2 * inv, 2 * H, 1)))
    wide = _dot(w.astype(BF16), ex_ref[...])
    o_dil = (wide[:, :W] * o0_ref[0] + wide[:, W:2 * W] * o1_ref[0] + wide[:, 2 * W:] * o2_ref[0]).astype(BF16)
    half = om_ref.shape[-1]
    mix = _dot(om_ref[0], wo_ref[:half, :]) + _dot(o_dil, wo_ref[half:, :])
    o_ref[0] = _layer_norm(DN_ALPHA * x_ref[0] + mix, g_ref[...], b_ref[...])


def _out_projection(o_mla, o_dils, lse_all, x, w_o, expand, g, b):
    B, S, _ = x.shape
    rows = min(PROJ_ROWS, S)
    half_spec = pl.BlockSpec((1, rows, o_mla.shape[-1]), lambda bi, i: (bi, i, 0))
    lse_spec = pl.BlockSpec((1, rows, LANES), lambda bi, i: (bi, i, 0))
    x_spec = pl.BlockSpec((1, rows, D_MODEL), lambda bi, i: (bi, i, 0))
    return pl.pallas_call(
        _out_proj_kernel,
        grid=(B, S // rows),
        in_specs=[half_spec, half_spec, half_spec, half_spec, lse_spec, x_spec, _const_spec(w_o.shape),
                  _const_spec(expand.shape), _const_spec(g.shape), _const_spec(b.shape)],
        out_specs=x_spec,
        out_shape=jax.ShapeDtypeStruct(x.shape, F32),
        compiler_params=_params("parallel", "parallel"),
        name="out_projection_ln",
    )(o_mla, *o_dils, lse_all, x, w_o, expand, g, b)


def _ffn_kernel(x_ref, halo_ref, wa_ref, wg_ref, cw_ref, wd_ref, g_ref, b_ref, o_ref, acc_ref):
    rows = x_ref.shape[1]
    pad = halo_ref.shape[1]
    x = x_ref[0]
    halo = jnp.where(pl.program_id(1) == 0, 0.0, halo_ref[0])
    xe = jnp.concatenate([halo, x], axis=0).astype(BF16)
    acc_ref[...] = jnp.zeros_like(acc_ref)

    def conv(u, taps):
        y = taps[3:4, :]
        for j in range(CONV_WIDTH):
            shift = CONV_WIDTH - 1 - j
            y = y + taps[j:j + 1, :] * u[pad - shift:pad - shift + rows, :]
        return y

    def body(c, carry):
        taps = cw_ref[c]
        ya = conv(_dot(xe, wa_ref[c]), taps[0:4])
        yg = conv(_dot(xe, wg_ref[c]), taps[4:8])
        hidden = (jax.nn.gelu(yg) * ya).astype(BF16)
        acc_ref[...] += _dot(hidden, wd_ref[c])
        return carry

    lax.fori_loop(0, wa_ref.shape[0], body, 0)
    o_ref[0] = _layer_norm(DN_ALPHA * x + acc_ref[...], g_ref[...], b_ref[...])


def _ffn(x1, wa, wg, cw, wd, g, b):
    B, S, _ = x1.shape
    rows = min(FFN_ROWS, S)
    pad = BF16_ROWS
    x_spec = pl.BlockSpec((1, rows, D_MODEL), lambda bi, i: (bi, i, 0))
    halo_spec = pl.BlockSpec((1, pad, D_MODEL), lambda bi, i: (bi, jnp.maximum(i * (rows // pad) - 1, 0), 0))
    return pl.pallas_call(
        _ffn_kernel,
        grid=(B, S // rows),
        in_specs=[x_spec, halo_spec, _const_spec(wa.shape), _const_spec(wg.shape), _const_spec(cw.shape),
                  _const_spec(wd.shape), _const_spec(g.shape), _const_spec(b.shape)],
        out_specs=x_spec,
        out_shape=jax.ShapeDtypeStruct(x1.shape, F32),
        scratch_shapes=[pltpu.VMEM((rows, D_MODEL), F32)],
        compiler_params=_params("parallel", "parallel"),
        name="conv_ffn_ln",
    )(x1, x1, wa, wg, cw, wd, g, b)


def _pad_cols(w, width):
    return jnp.pad(w, ((0, 0), (0, width - w.shape[1])))


def _head_groups(w):
    rank, heads, e = w.shape
    return jnp.pad(w, ((0, 0), (0, 0), (0, LANES - e))).reshape(rank, heads * LANES)


def _prepare(w_in, g_cq, g_ckv, w_uq, w_uk, w_uv, w_o, ln1_g, ln1_b, w_up, conv_w, conv_b, w_down, ln2_g, ln2_b):
    r0, r1, r2 = MLA_Q_RANK, MLA_Q_RANK + MLA_KV_RANK, MLA_Q_RANK + MLA_KV_RANK + MLA_ROPE_DIM
    w_in_p = jnp.concatenate([w_in[:, :r1], _pad_cols(w_in[:, r1:r2], LANES), w_in[:, r2:]], axis=1).astype(BF16)

    half = MLA_ROPE_DIM // 2
    rope = w_uq[:, :, MLA_NOPE_DIM:]
    swapped = jnp.concatenate([-rope[:, :, half:], rope[:, :, :half]], axis=-1)
    swapped = jnp.concatenate([jnp.zeros_like(w_uq[:, :, :MLA_NOPE_DIM]), swapped], axis=-1)
    wq2 = jnp.concatenate([_head_groups(w_uq), _head_groups(swapped)], axis=1).astype(BF16)

    wk = _head_groups(w_uk).astype(BF16)
    wvt = jnp.pad(w_uv.transpose(1, 2, 0), ((0, 0), (0, LANES - HEAD_DIM), (0, 0))).astype(BF16)
    vone = jnp.zeros((LANES, 1), F32).at[HEAD_DIM, 0].set(1.0)

    n_chunks = D_FF // FFN_CHUNK
    wa = w_up[:, :D_FF].reshape(D_MODEL, n_chunks, FFN_CHUNK).transpose(1, 0, 2).astype(BF16)
    wg = w_up[:, D_FF:].reshape(D_MODEL, n_chunks, FFN_CHUNK).transpose(1, 0, 2).astype(BF16)
    taps = jnp.concatenate([conv_w, conv_b[None, :]], axis=0)
    cw = jnp.concatenate([taps[:, :D_FF].reshape(4, n_chunks, FFN_CHUNK),
                          taps[:, D_FF:].reshape(4, n_chunks, FFN_CHUNK)], axis=0).transpose(1, 0, 2)
    wd = w_down.reshape(n_chunks, FFN_CHUNK, D_MODEL).astype(BF16)
    src = jnp.arange(LANES)[:, None]
    dst = jnp.arange(len(DIL_PAIRS) * DIL_WIDTH)[None, :]
    expand = (src == dst // HEAD_DIM).astype(BF16)
    row = lambda a: a.reshape(1, -1)
    return dict(w_in_p=w_in_p, gq=row(g_cq), gkv=row(g_ckv), wq2=wq2, wk=wk, wvt=wvt, vone=vone,
                w_o=w_o.astype(BF16), expand=expand, ln1=(row(ln1_g), row(ln1_b)), wa=wa, wg=wg, cw=cw, wd=wd,
                ln2=(row(ln2_g), row(ln2_b)))


def _freq_lanes():
    half = MLA_ROPE_DIM // 2
    freqs = ROPE_THETA ** (-jnp.arange(half, dtype=F32) / half)
    zeros = jnp.zeros((MLA_NOPE_DIM,), F32)
    return jnp.concatenate([zeros, freqs, freqs, jnp.zeros((LANES - MLA_NOPE_DIM - MLA_ROPE_DIM,), F32)])[None, :]


def kernel(x, w_in, g_cq, g_ckv, w_uq, w_uk, w_uv, w_o, ln1_g, ln1_b, w_up, conv_w, conv_b, w_down, ln2_g, ln2_b):
    B, S, _ = x.shape
    p = _prepare(w_in, g_cq, g_ckv, w_uq, w_uk, w_uv, w_o, ln1_g, ln1_b, w_up, conv_w, conv_b, w_down, ln2_g, ln2_b)
    scale = math.log2(math.e) / math.sqrt(MLA_NOPE_DIM + MLA_ROPE_DIM)
    tables = _rope_tables(S, _freq_lanes(), scale)
    q, k, vt, qkv_dil = _projection(x, p["w_in_p"], p["gq"], p["gkv"], p["wq2"], p["wk"], p["wvt"], tables,
                                    p["vone"])
    o_mla = _mla_attention(q, k, vt)
    o_dils, lse_all = _dilated_attention(qkv_dil)
    x1 = _out_projection(o_mla, o_dils, lse_all, x, p["w_o"], p["expand"], *p["ln1"])
    return _ffn(x1, p["wa"], p["wg"], p["cw"], p["wd"], *p["ln2"])
```

```python
import functools
import math

import jax
import jax.numpy as jnp
from jax import lax
from jax.experimental import pallas as pl
from jax.experimental.pallas import tpu as pltpu

D_MODEL = 1024
HEAD_DIM = 64
MLA_HEADS = 8
MLA_Q_RANK = 256
MLA_KV_RANK = 128
MLA_NOPE_DIM = 64
MLA_ROPE_DIM = 32
ROPE_THETA = 10000.0
DIL_HEADS = 8
DIL_PAIRS = ((128, 1), (512, 4), (2048, 16))
DIL_BLOCK = 128
DIL_WIDTH = DIL_HEADS * HEAD_DIM
D_FF = 2816
CONV_WIDTH = 3
DEPTH = 1
DN_ALPHA = (2.0 * DEPTH) ** 0.25
LN_EPS = 1e-5
RMS_EPS = 1e-6

LANES = 128
BF16_ROWS = 16
VMEM_LIMIT = 56 * 1024 * 1024

PROJ_ROWS = 512
DIL_SPAN = 2048
FFN_ROWS = 512
FFN_CHUNK = 256
NEG_BIG = 1e30

BF16 = jnp.bfloat16
F32 = jnp.float32


def _dot(a, b):
    return jnp.dot(a, b, preferred_element_type=F32)


def _dot_nt(a, b):
    return lax.dot_general(a, b, (((1,), (1,)), ((), ())), preferred_element_type=F32)


def _params(*sem, flags=None):
    return pltpu.CompilerParams(dimension_semantics=sem, vmem_limit_bytes=VMEM_LIMIT, flags=flags)


def _const_spec(shape):
    zeros = (0,) * len(shape)
    return pl.BlockSpec(shape, lambda *_: zeros, pipeline_mode=pl.Buffered(1))


def _rope_table_kernel(freq_ref, out_ref, *, scale):
    rows = out_ref.shape[1]
    pos = lax.broadcasted_iota(jnp.int32, (rows, LANES), 0).astype(F32)
    ang = pos * freq_ref[...]
    c = jnp.cos(ang)
    s = jnp.sin(ang)
    out_ref[0] = c
    out_ref[1] = s
    out_ref[2] = c * scale
    out_ref[3] = s * scale


def _rope_tables(seq, freq_lanes, scale):
    return pl.pallas_call(
        functools.partial(_rope_table_kernel, scale=scale),
        out_shape=jax.ShapeDtypeStruct((4, seq, LANES), F32),
        name="rope_tables",
    )(freq_lanes)


def _rms(x, g):
    ms = jnp.mean(x * x, axis=-1, keepdims=True)
    return x * lax.rsqrt(ms + RMS_EPS) * g


def _proj_kernel(x_ref, w_in_ref, gq_ref, gkv_ref, wq_ref, wk_ref, wvt_ref, tab_ref, vone_ref,
                 q_ref, k_ref, vt_ref, *rest):
    dil_refs, hd_ref = rest[:-1], rest[-1]
    rows = x_ref.shape[1]
    xb = x_ref[0].astype(BF16)
    h = _dot(xb, w_in_ref[...])
    cq = _rms(h[:, :MLA_Q_RANK], gq_ref[...]).astype(BF16)
    ckv = _rms(h[:, MLA_Q_RANK:MLA_Q_RANK + MLA_KV_RANK], gkv_ref[...]).astype(BF16)
    kr = h[:, 3 * LANES:4 * LANES]
    cos, sin, cos_q, sin_q = tab_ref[0], tab_ref[1], tab_ref[2], tab_ref[3]

    k_plain = pltpu.roll(kr, 64, 1)
    k_swap = pltpu.roll(kr, 80, 1) - pltpu.roll(kr, 48, 1)
    k_rope = k_plain * cos + k_swap * sin

    q2 = _dot(cq, wq_ref[...])
    kn = _dot(ckv, wk_ref[...])
    hw = MLA_HEADS * LANES
    for hd in range(MLA_HEADS):
        lo = hd * LANES
        qh = q2[:, lo:lo + LANES] * cos_q + q2[:, hw + lo:hw + lo + LANES] * sin_q
        q_ref[0, hd] = qh.astype(BF16)
        k_ref[0, hd] = (kn[:, lo:lo + LANES] + k_rope).astype(BF16)
        vt_ref[0, hd, 0] = (_dot_nt(wvt_ref[hd], ckv) + vone_ref[...]).astype(BF16)

    base = 4 * LANES
    groups = DIL_WIDTH // LANES
    for c in range(3 * groups):
        slab = h[:, base + c * LANES:base + (c + 1) * LANES]
        hd_ref[c] = slab * (math.log2(math.e) / math.sqrt(HEAD_DIM)) if c < groups else slab
    for di, (_, dil) in enumerate(DIL_PAIRS):
        for part in range(3):
            out = dil_refs[3 * di + part]
            for r in range(dil):
                for g in range(groups):
                    piece = hd_ref[part * groups + g, pl.ds(r, rows // dil, stride=dil), :]
                    out[0, r, :, g * LANES:(g + 1) * LANES] = piece.astype(BF16)


def _projection(x, w_in_p, gq, gkv, wq2, wk, wvt, tables, vone):
    B, S, _ = x.shape
    rows = min(PROJ_ROWS, S)
    head_shape = jax.ShapeDtypeStruct((B, MLA_HEADS, S, LANES), BF16)
    vt_shape = jax.ShapeDtypeStruct((B, MLA_HEADS, S // rows, LANES, rows), BF16)
    head_spec = pl.BlockSpec((1, MLA_HEADS, rows, LANES), lambda b, i: (b, 0, i, 0))
    vt_spec = pl.BlockSpec((1, MLA_HEADS, 1, LANES, rows), lambda b, i: (b, 0, i, 0, 0))
    dil_shapes, dil_specs = [], []
    for _, dil in DIL_PAIRS:
        assert rows % (dil * BF16_ROWS) == 0
        dil_shapes += [jax.ShapeDtypeStruct((B, dil, S // dil, DIL_WIDTH), BF16)] * 3
        dil_specs += [pl.BlockSpec((1, dil, rows // dil, DIL_WIDTH), lambda b, i: (b, 0, i, 0))] * 3
    outs = pl.pallas_call(
        _proj_kernel,
        grid=(B, S // rows),
        in_specs=[
            pl.BlockSpec((1, rows, D_MODEL), lambda b, i: (b, i, 0)),
            _const_spec(w_in_p.shape),
            _const_spec(gq.shape),
            _const_spec(gkv.shape),
            _const_spec(wq2.shape),
            _const_spec(wk.shape),
            _const_spec(wvt.shape),
            pl.BlockSpec((4, rows, LANES), lambda b, i: (0, i, 0)),
            _const_spec(vone.shape),
        ],
        out_specs=[head_spec, head_spec, vt_spec] + dil_specs,
        out_shape=[head_shape, head_shape, vt_shape] + dil_shapes,
        scratch_shapes=[pltpu.VMEM((3 * DIL_WIDTH // LANES, rows, LANES), F32)],
        compiler_params=_params("parallel", "parallel"),
        name="in_projection",
    )(x, w_in_p, gq, gkv, wq2, wk, wvt, tables, vone)
    return outs[0], outs[1], outs[2], [outs[3 + 3 * d:6 + 3 * d] for d in range(len(DIL_PAIRS))]


def _mla_kernel(q_ref, k_ref, vt_ref, o_ref, acc_ref, m_ref, st_ref, *, blk):
    qi = pl.program_id(2)
    acc_ref[...] = jnp.zeros_like(acc_ref)
    m_ref[...] = jnp.full_like(m_ref, -NEG_BIG)

    def scores(hh, j):
        start = pl.multiple_of(j * blk, blk)
        return _dot_nt(k_ref[0, hh, pl.ds(start, blk), :], q_ref[0, hh])

    def update(hh, j, st, masked):
        if masked:
            key = lax.broadcasted_iota(jnp.int32, (blk, blk), 0)
            qry = lax.broadcasted_iota(jnp.int32, (blk, blk), 1)
            st = jnp.where(key <= qry, st, -NEG_BIG)
        m_old = m_ref[hh]
        m_new = jnp.maximum(m_old, jnp.max(st, axis=0, keepdims=True))
        pt = jnp.exp2(st - m_new).astype(BF16)
        acc_ref[hh] = acc_ref[hh] * jnp.exp2(m_old - m_new) + _dot(vt_ref[0, hh, j], pt)
        m_ref[hh] = m_new

    st_ref[...] = scores(0, 0)

    def body(j, carry):
        st_a = st_ref[...]
        st_b = scores(1, j)
        update(0, j, st_a, False)
        st_a_next = scores(0, j + 1)
        update(1, j, st_b, False)
        st_ref[...] = st_a_next
        return carry

    lax.fori_loop(0, qi, body, 0)
    st_a = st_ref[...]
    st_b = scores(1, qi)
    update(0, qi, st_a, True)
    update(1, qi, st_b, True)
    outs = []
    for hh in range(2):
        acc = acc_ref[hh]
        outs.append(acc[:HEAD_DIM] / acc[HEAD_DIM:HEAD_DIM + 1])
    o_ref[0] = jnp.concatenate(outs, axis=0).T.astype(o_ref.dtype)


def _mla_attention(q, k, vt):
    B, H, S, _ = q.shape
    blk = vt.shape[-1]
    k_spec = pl.BlockSpec((1, 2, S, LANES), lambda b, hp, i: (b, hp, 0, 0))
    vt_spec = pl.BlockSpec((1, 2, S // blk, LANES, blk), lambda b, hp, i: (b, hp, 0, 0, 0))
    return pl.pallas_call(
        functools.partial(_mla_kernel, blk=blk),
        grid=(B, H // 2, S // blk),
        in_specs=[pl.BlockSpec((1, 2, blk, LANES), lambda b, hp, i: (b, hp, i, 0)), k_spec, vt_spec],
        out_specs=pl.BlockSpec((1, blk, LANES), lambda b, hp, i: (b, i, hp)),
        out_shape=jax.ShapeDtypeStruct((B, S, H * HEAD_DIM), BF16),
        scratch_shapes=[pltpu.VMEM((2, LANES, blk), F32), pltpu.VMEM((2, 1, blk), F32),
                        pltpu.VMEM((blk, blk), F32)],
        compiler_params=_params("parallel", "parallel", "arbitrary"),
        name="mla_attention",
    )(q, k, vt)


def _dil_bias_tiles(bias_ref, dil):
    blk = DIL_BLOCK
    key = lax.broadcasted_iota(jnp.int32, (2 * blk, blk), 0)
    qry = lax.broadcasted_iota(jnp.int32, (2 * blk, blk), 1)
    off = qry + blk - key
    steps = jnp.where((off >= 0) & (off <= blk), off.astype(F32), NEG_BIG)
    for pair in range(DIL_HEADS // 2):
        halves = []
        for hh in range(2):
            slope = 2.0 ** (-8.0 * (2 * pair + hh + 1) / DIL_HEADS)
            halves.append(steps * (-slope * dil * math.log2(math.e)))
        bias_ref[pair] = jnp.concatenate(halves, axis=1)


def _dil_kernel(q_ref, kc_ref, kp_ref, vc_ref, vp_ref, o_ref, lse_ref, k_buf, v_buf, st_scr, o_scr, bias_ref, *, dil):
    blk = DIL_BLOCK
    pairs = DIL_HEADS // 2
    nb = q_ref.shape[2] // blk
    units = dil * nb
    _dil_bias_tiles(bias_ref, dil)
    k_buf[:, :blk, :] = kp_ref[0]
    k_buf[:, blk:, :] = kc_ref[0]
    v_buf[:, :blk, :] = vp_ref[0]
    v_buf[:, blk:, :] = vc_ref[0]
    first_span = pl.program_id(1) == 0
    lane = lax.broadcasted_iota(jnp.int32, (blk, LANES), 1)
    first = lane < HEAD_DIM

    def split(u):
        if isinstance(u, int) or nb == 1:
            return u // nb, u % nb if nb > 1 else 0
        return lax.shift_right_logical(u, nb.bit_length() - 1), u & (nb - 1)

    def row0(i):
        return i * blk if isinstance(i, int) else pl.multiple_of(i * blk, blk)

    def scores(u):
        r, i = split(u)
        q = q_ref[0, r, pl.ds(row0(i), blk), :]
        keys = pl.ds(row0(i), 2 * blk)
        tiles = []
        for pair in range(pairs):
            cols = slice(pair * LANES, (pair + 1) * LANES)
            qp = q[:, cols]
            zero = jnp.zeros_like(qp)
            q2 = jnp.concatenate([jnp.where(first, qp, zero), jnp.where(first, zero, qp)], axis=0)
            tiles.append(_dot_nt(k_buf[r, keys, cols], q2))
        return tiles

    def finish(u, tiles):
        r, i = split(u)
        keys = pl.ds(row0(i), 2 * blk)
        pen = jnp.where(jnp.logical_and(first_span, i == 0), -NEG_BIG, 0.0)
        if dil == 1:
            rows = pl.ds(row0(i), blk)
        else:
            rows = pl.ds(r + i * (blk * dil), blk, stride=dil)
        for pair in range(pairs):
            cols = slice(pair * LANES, (pair + 1) * LANES)
            st = tiles[pair] + bias_ref[pair]
            st = jnp.concatenate([st[:blk] + pen, st[blk:]], axis=0)
            m = jnp.max(st, axis=0, keepdims=True)
            pt = jnp.exp2(st - m)
            l = jnp.sum(pt, axis=0, keepdims=True)
            res = lax.dot_general(v_buf[r, keys, cols], pt.astype(BF16), (((0,), (0,)), ((), ())),
                                  preferred_element_type=F32)
            inv = 1.0 / l
            ot = jnp.concatenate([res[:HEAD_DIM, :LANES] * inv[:, :LANES],
                                  res[HEAD_DIM:, LANES:] * inv[:, LANES:]], axis=0)
            o_scr[pair, rows, :] = ot.T
            lse2 = m + jnp.log2(l)
            lse_ref[0, r, i, 2 * pair:2 * pair + 1, :] = lse2[:, :LANES]
            lse_ref[0, r, i, 2 * pair + 1:2 * pair + 2, :] = lse2[:, LANES:]

    for pair, tile in enumerate(scores(0)):
        st_scr[pair] = tile

    def body(u, carry):
        tiles = [st_scr[pair] for pair in range(pairs)]
        ahead = scores(u + 1)
        finish(u, tiles)
        for pair, tile in enumerate(ahead):
            st_scr[pair] = tile
        return carry

    lax.fori_loop(0, units - 1, body, 0)
    finish(units - 1, [st_scr[pair] for pair in range(pairs)])
    for pair in range(pairs):
        o_ref[0, :, pair * LANES:(pair + 1) * LANES] = o_scr[pair].astype(o_ref.dtype)


def _dil_branch(q, k, v, dil):
    B, _, M, W = q.shape
    S = M * dil
    span = min(DIL_SPAN, S)
    assert span % (dil * DIL_BLOCK) == 0 and S % span == 0
    rows = span // dil
    nb = rows // DIL_BLOCK
    cur = pl.BlockSpec((1, dil, rows, W), lambda b, c: (b, 0, c, 0))
    prev = pl.BlockSpec((1, dil, DIL_BLOCK, W), lambda b, c: (b, 0, jnp.maximum(c * nb - 1, 0), 0))
    o, lse = pl.pallas_call(
        functools.partial(_dil_kernel, dil=dil),
        grid=(B, S // span),
        in_specs=[cur, cur, prev, cur, prev],
        out_specs=[pl.BlockSpec((1, span, W), lambda b, c: (b, c, 0)),
                   pl.BlockSpec((1, dil, nb, DIL_HEADS, LANES), lambda b, c: (b, 0, c, 0, 0))],
        out_shape=[jax.ShapeDtypeStruct((B, S, W), BF16),
                   jax.ShapeDtypeStruct((B, dil, M // DIL_BLOCK, DIL_HEADS, LANES), F32)],
        scratch_shapes=[pltpu.VMEM((dil, DIL_BLOCK + rows, W), BF16),
                        pltpu.VMEM((dil, DIL_BLOCK + rows, W), BF16),
                        pltpu.VMEM((DIL_HEADS // 2, 2 * DIL_BLOCK, 2 * LANES), F32),
                        pltpu.VMEM((DIL_HEADS // 2, span, LANES), F32),
                        pltpu.VMEM((DIL_HEADS // 2, 2 * DIL_BLOCK, 2 * LANES), F32)],
        compiler_params=_params("parallel", "arbitrary"),
        name=f"dilated_d{dil}",
    )(q, k, k, v, v)
    return o, lse.transpose(0, 2, 4, 1, 3).reshape(B, S, DIL_HEADS)


def _dilated_attention(qkv_per_dil):
    outs, lses = [], []
    for (window, dil), (q, k, v) in zip(DIL_PAIRS, qkv_per_dil):
        assert window // dil == DIL_BLOCK
        o, lse = _dil_branch(q, k, v, dil)
        outs.append(o)
        lses.append(lse)
    lse_all = jnp.concatenate(lses, axis=-1)
    return outs, jnp.pad(lse_all, ((0, 0), (0, 0), (0, LANES - lse_all.shape[-1])))


def _layer_norm(y, g, b):
    mu = jnp.mean(y, axis=-1, keepdims=True)
    d = y - mu
    var = jnp.mean(d * d, axis=-1, keepdims=True)
    return d * lax.rsqrt(var + LN_EPS) * g + b


def _out_proj_kernel(om_ref, o0_ref, o1_ref, o2_ref, lse_ref, x_ref, wo_ref, ex_ref, g_ref, b_ref, o_ref):
    H, W = DIL_HEADS, DIL_WIDTH
    l0 = lse_ref[0]
    l1 = pltpu.roll(l0, LANES - H, 1)
    l2 = pltpu.roll(l0, LANES - 2 * H, 1)
    top = jnp.maximum(l0, jnp.maximum(l1, l2))
    e0, e1, e2 = jnp.exp2(l0 - top), jnp.exp2(l1 - top), jnp.exp2(l2 - top)
    inv = 1.0 / (e0 + e1 + e2)
    lane = lax.broadcasted_iota(jnp.int32, l0.shape, 1)
    w = jnp.where(lane < H, e0 * inv,
                  jnp.where(lane < 2 * H, pltpu.roll(e1 * inv, H, 1), pltpu.roll(e2 * inv, 2 * H, 1)))
    wide = _dot(w.astype(BF16), ex_ref[...])
    o_dil = (wide[:, :W] * o0_ref[0] + wide[:, W:2 * W] * o1_ref[0] + wide[:, 2 * W:] * o2_ref[0]).astype(BF16)
    half = om_ref.shape[-1]
    mix = _dot(om_ref[0], wo_ref[:half, :]) + _dot(o_dil, wo_ref[half:, :])
    o_ref[0] = _layer_norm(DN_ALPHA * x_ref[0] + mix, g_ref[...], b_ref[...])


def _out_projection(o_mla, o_dils, lse_all, x, w_o, expand, g, b):
    B, S, _ = x.shape
    rows = min(PROJ_ROWS, S)
    half_spec = pl.BlockSpec((1, rows, o_mla.shape[-1]), lambda bi, i: (bi, i, 0))
    lse_spec = pl.BlockSpec((1, rows, LANES), lambda bi, i: (bi, i, 0))
    x_spec = pl.BlockSpec((1, rows, D_MODEL), lambda bi, i: (bi, i, 0))
    return pl.pallas_call(
        _out_proj_kernel,
        grid=(B, S // rows),
        in_specs=[half_spec, half_spec, half_spec, half_spec, lse_spec, x_spec, _const_spec(w_o.shape),
                  _const_spec(expand.shape), _const_spec(g.shape), _const_spec(b.shape)],
        out_specs=x_spec,
        out_shape=jax.ShapeDtypeStruct(x.shape, F32),
        compiler_params=_params("parallel", "parallel"),
        name="out_projection_ln",
    )(o_mla, *o_dils, lse_all, x, w_o, expand, g, b)


def _ffn_kernel(x_ref, halo_ref, wa_ref, wg_ref, cw_ref, wd_ref, g_ref, b_ref, o_ref, acc_ref, u0_ref, u1_ref):
    rows = x_ref.shape[1]
    pad = halo_ref.shape[1]
    chunks = wa_ref.shape[0]
    x = x_ref[0]
    halo = jnp.where(pl.program_id(1) == 0, 0.0, halo_ref[0])
    xe = jnp.concatenate([halo, x], axis=0).astype(BF16)
    acc_ref[...] = jnp.zeros_like(acc_ref)

    def up(c, u_ref):
        u_ref[0] = _dot(xe, wa_ref[c])
        u_ref[1] = _dot(xe, wg_ref[c])

    def conv(u_ref, part, taps):
        y = taps[3:4, :]
        for j in range(CONV_WIDTH):
            shift = CONV_WIDTH - 1 - j
            y = y + taps[j:j + 1, :] * u_ref[part, pl.ds(pad - shift, rows), :]
        return y

    def down(c, u_ref):
        taps = cw_ref[c]
        ya = conv(u_ref, 0, taps[0:4])
        yg = conv(u_ref, 1, taps[4:8])
        hidden = (jax.nn.gelu(yg) * ya).astype(BF16)
        acc_ref[...] += _dot(hidden, wd_ref[c])

    assert chunks % 2 == 1
    up(0, u0_ref)

    def body(j, carry):
        c = 2 * j
        up(c + 1, u1_ref)
        down(c, u0_ref)
        up(c + 2, u0_ref)
        down(c + 1, u1_ref)
        return carry

    lax.fori_loop(0, chunks // 2, body, 0)
    down(chunks - 1, u0_ref)
    o_ref[0] = _layer_norm(DN_ALPHA * x + acc_ref[...], g_ref[...], b_ref[...])


def _ffn(x1, wa, wg, cw, wd, g, b):
    B, S, _ = x1.shape
    rows = min(FFN_ROWS, S)
    pad = BF16_ROWS
    x_spec = pl.BlockSpec((1, rows, D_MODEL), lambda bi, i: (bi, i, 0))
    halo_spec = pl.BlockSpec((1, pad, D_MODEL), lambda bi, i: (bi, jnp.maximum(i * (rows // pad) - 1, 0), 0))
    return pl.pallas_call(
        _ffn_kernel,
        grid=(B, S // rows),
        in_specs=[x_spec, halo_spec, _const_spec(wa.shape), _const_spec(wg.shape), _const_spec(cw.shape),
                  _const_spec(wd.shape), _const_spec(g.shape), _const_spec(b.shape)],
        out_specs=x_spec,
        out_shape=jax.ShapeDtypeStruct(x1.shape, F32),
        scratch_shapes=[pltpu.VMEM((rows, D_MODEL), F32),
                        pltpu.VMEM((2, pad + rows, FFN_CHUNK), F32), pltpu.VMEM((2, pad + rows, FFN_CHUNK), F32)],
        compiler_params=_params("parallel", "parallel"),
        name="conv_ffn_ln",
    )(x1, x1, wa, wg, cw, wd, g, b)


def _pad_cols(w, width):
    return jnp.pad(w, ((0, 0), (0, width - w.shape[1])))


def _head_groups(w):
    rank, heads, e = w.shape
    return jnp.pad(w, ((0, 0), (0, 0), (0, LANES - e))).reshape(rank, heads * LANES)


def _prepare(w_in, g_cq, g_ckv, w_uq, w_uk, w_uv, w_o, ln1_g, ln1_b, w_up, conv_w, conv_b, w_down, ln2_g, ln2_b):
    r0, r1, r2 = MLA_Q_RANK, MLA_Q_RANK + MLA_KV_RANK, MLA_Q_RANK + MLA_KV_RANK + MLA_ROPE_DIM
    w_in_p = jnp.concatenate([w_in[:, :r1], _pad_cols(w_in[:, r1:r2], LANES), w_in[:, r2:]], axis=1).astype(BF16)

    half = MLA_ROPE_DIM // 2
    rope = w_uq[:, :, MLA_NOPE_DIM:]
    swapped = jnp.concatenate([-rope[:, :, half:], rope[:, :, :half]], axis=-1)
    swapped = jnp.concatenate([jnp.zeros_like(w_uq[:, :, :MLA_NOPE_DIM]), swapped], axis=-1)
    wq2 = jnp.concatenate([_head_groups(w_uq), _head_groups(swapped)], axis=1).astype(BF16)

    wk = _head_groups(w_uk).astype(BF16)
    wvt = jnp.pad(w_uv.transpose(1, 2, 0), ((0, 0), (0, LANES - HEAD_DIM), (0, 0))).astype(BF16)
    vone = jnp.zeros((LANES, 1), F32).at[HEAD_DIM, 0].set(1.0)

    n_chunks = D_FF // FFN_CHUNK
    wa = w_up[:, :D_FF].reshape(D_MODEL, n_chunks, FFN_CHUNK).transpose(1, 0, 2).astype(BF16)
    wg = w_up[:, D_FF:].reshape(D_MODEL, n_chunks, FFN_CHUNK).transpose(1, 0, 2).astype(BF16)
    taps = jnp.concatenate([conv_w, conv_b[None, :]], axis=0)
    cw = jnp.concatenate([taps[:, :D_FF].reshape(4, n_chunks, FFN_CHUNK),
                          taps[:, D_FF:].reshape(4, n_chunks, FFN_CHUNK)], axis=0).transpose(1, 0, 2)
    wd = w_down.reshape(n_chunks, FFN_CHUNK, D_MODEL).astype(BF16)
    src = jnp.arange(LANES)[:, None]
    dst = jnp.arange(len(DIL_PAIRS) * DIL_WIDTH)[None, :]
    expand = (src == dst // HEAD_DIM).astype(BF16)
    row = lambda a: a.reshape(1, -1)
    return dict(w_in_p=w_in_p, gq=row(g_cq), gkv=row(g_ckv), wq2=wq2, wk=wk, wvt=wvt, vone=vone,
                w_o=w_o.astype(BF16), expand=expand, ln1=(row(ln1_g), row(ln1_b)), wa=wa, wg=wg, cw=cw, wd=wd,
                ln2=(row(ln2_g), row(ln2_b)))


def _freq_lanes():
    half = MLA_ROPE_DIM // 2
    freqs = ROPE_THETA ** (-jnp.arange(half, dtype=F32) / half)
    zeros = jnp.zeros((MLA_NOPE_DIM,), F32)
    return jnp.concatenate([zeros, freqs, freqs, jnp.zeros((LANES - MLA_NOPE_DIM - MLA_ROPE_DIM,), F32)])[None, :]


def kernel(x, w_in, g_cq, g_ckv, w_uq, w_uk, w_uv, w_o, ln1_g, ln1_b, w_up, conv_w, conv_b, w_down, ln2_g, ln2_b):
    B, S, _ = x.shape
    p = _prepare(w_in, g_cq, g_ckv, w_uq, w_uk, w_uv, w_o, ln1_g, ln1_b, w_up, conv_w, conv_b, w_down, ln2_g, ln2_b)
    scale = math.log2(math.e) / math.sqrt(MLA_NOPE_DIM + MLA_ROPE_DIM)
    tables = _rope_tables(S, _freq_lanes(), scale)
    q, k, vt, qkv_dil = _projection(x, p["w_in_p"], p["gq"], p["gkv"], p["wq2"], p["wk"], p["wvt"], tables,
                                    p["vone"])
    o_mla = _mla_attention(q, k, vt)
    o_dils, lse_all = _dilated_attention(qkv_dil)
    x1 = _out_projection(o_mla, o_dils, lse_all, x, p["w_o"], p["expand"], *p["ln1"])
    return _ffn(x1, p["wa"], p["wg"], p["cw"], p["wd"], *p["ln2"])
```

```python
import functools
import math

import jax
import jax.numpy as jnp
from jax import lax
from jax.experimental import pallas as pl
from jax.experimental.pallas import tpu as pltpu

D_MODEL = 1024
HEAD_DIM = 64
MLA_HEADS = 8
MLA_Q_RANK = 256
MLA_KV_RANK = 128
MLA_NOPE_DIM = 64
MLA_ROPE_DIM = 32
ROPE_THETA = 10000.0
DIL_HEADS = 8
DIL_PAIRS = ((128, 1), (512, 4), (2048, 16))
DIL_BLOCK = 128
DIL_WIDTH = DIL_HEADS * HEAD_DIM
D_FF = 2816
CONV_WIDTH = 3
DEPTH = 1
DN_ALPHA = (2.0 * DEPTH) ** 0.25
LN_EPS = 1e-5
RMS_EPS = 1e-6

LANES = 128
BF16_ROWS = 16
VMEM_LIMIT = 56 * 1024 * 1024

PROJ_ROWS = 512
DIL_SPAN = 2048
FFN_ROWS = 512
FFN_CHUNK = 256
NEG_BIG = 1e30

BF16 = jnp.bfloat16
F32 = jnp.float32


def _dot(a, b):
    return jnp.dot(a, b, preferred_element_type=F32)


def _dot_nt(a, b):
    return lax.dot_general(a, b, (((1,), (1,)), ((), ())), preferred_element_type=F32)


def _params(*sem, flags=None):
    return pltpu.CompilerParams(dimension_semantics=sem, vmem_limit_bytes=VMEM_LIMIT, flags=flags)


def _const_spec(shape):
    zeros = (0,) * len(shape)
    return pl.BlockSpec(shape, lambda *_: zeros, pipeline_mode=pl.Buffered(1))


def _rope_table_kernel(freq_ref, out_ref, *, scale):
    rows = out_ref.shape[1]
    pos = lax.broadcasted_iota(jnp.int32, (rows, LANES), 0).astype(F32)
    ang = pos * freq_ref[...]
    c = jnp.cos(ang)
    s = jnp.sin(ang)
    out_ref[0] = c
    out_ref[1] = s
    out_ref[2] = c * scale
    out_ref[3] = s * scale


def _rope_tables(seq, freq_lanes, scale):
    return pl.pallas_call(
        functools.partial(_rope_table_kernel, scale=scale),
        out_shape=jax.ShapeDtypeStruct((4, seq, LANES), F32),
        name="rope_tables",
    )(freq_lanes)


def _rms(x, g):
    ms = jnp.mean(x * x, axis=-1, keepdims=True)
    return x * lax.rsqrt(ms + RMS_EPS) * g


def _proj_kernel(x_ref, w_in_ref, gq_ref, gkv_ref, wq_ref, wk_ref, wvt_ref, tab_ref, vone_ref,
                 q_ref, k_ref, vt_ref, *rest):
    dil_refs, hd_ref = rest[:-1], rest[-1]
    rows = x_ref.shape[1]
    xb = x_ref[0].astype(BF16)
    h = _dot(xb, w_in_ref[...])
    cq = _rms(h[:, :MLA_Q_RANK], gq_ref[...]).astype(BF16)
    ckv = _rms(h[:, MLA_Q_RANK:MLA_Q_RANK + MLA_KV_RANK], gkv_ref[...]).astype(BF16)
    kr = h[:, 3 * LANES:4 * LANES]
    cos, sin, cos_q, sin_q = tab_ref[0], tab_ref[1], tab_ref[2], tab_ref[3]

    k_plain = pltpu.roll(kr, 64, 1)
    k_swap = pltpu.roll(kr, 80, 1) - pltpu.roll(kr, 48, 1)
    k_rope = k_plain * cos + k_swap * sin

    q2 = _dot(cq, wq_ref[...])
    kn = _dot(ckv, wk_ref[...])
    hw = MLA_HEADS * LANES
    for hd in range(MLA_HEADS):
        lo = hd * LANES
        qh = q2[:, lo:lo + LANES] * cos_q + q2[:, hw + lo:hw + lo + LANES] * sin_q
        q_ref[0, hd] = qh.astype(BF16)
        k_ref[0, hd] = (kn[:, lo:lo + LANES] + k_rope).astype(BF16)
        vt_ref[0, hd, 0] = (_dot_nt(wvt_ref[hd], ckv) + vone_ref[...]).astype(BF16)

    base = 4 * LANES
    groups = DIL_WIDTH // LANES
    for c in range(3 * groups):
        slab = h[:, base + c * LANES:base + (c + 1) * LANES]
        hd_ref[c] = slab * (math.log2(math.e) / math.sqrt(HEAD_DIM)) if c < groups else slab
    for di, (_, dil) in enumerate(DIL_PAIRS):
        for part in range(3):
            out = dil_refs[3 * di + part]
            for r in range(dil):
                for g in range(groups):
                    piece = hd_ref[part * groups + g, pl.ds(r, rows // dil, stride=dil), :]
                    out[0, r, :, g * LANES:(g + 1) * LANES] = piece.astype(BF16)


def _projection(x, w_in_p, gq, gkv, wq2, wk, wvt, tables, vone):
    B, S, _ = x.shape
    rows = min(PROJ_ROWS, S)
    head_shape = jax.ShapeDtypeStruct((B, MLA_HEADS, S, LANES), BF16)
    vt_shape = jax.ShapeDtypeStruct((B, MLA_HEADS, S // rows, LANES, rows), BF16)
    head_spec = pl.BlockSpec((1, MLA_HEADS, rows, LANES), lambda b, i: (b, 0, i, 0))
    vt_spec = pl.BlockSpec((1, MLA_HEADS, 1, LANES, rows), lambda b, i: (b, 0, i, 0, 0))
    dil_shapes, dil_specs = [], []
    for _, dil in DIL_PAIRS:
        assert rows % (dil * BF16_ROWS) == 0
        dil_shapes += [jax.ShapeDtypeStruct((B, dil, S // dil, DIL_WIDTH), BF16)] * 3
        dil_specs += [pl.BlockSpec((1, dil, rows // dil, DIL_WIDTH), lambda b, i: (b, 0, i, 0))] * 3
    outs = pl.pallas_call(
        _proj_kernel,
        grid=(B, S // rows),
        in_specs=[
            pl.BlockSpec((1, rows, D_MODEL), lambda b, i: (b, i, 0)),
            _const_spec(w_in_p.shape),
            _const_spec(gq.shape),
            _const_spec(gkv.shape),
            _const_spec(wq2.shape),
            _const_spec(wk.shape),
            _const_spec(wvt.shape),
            pl.BlockSpec((4, rows, LANES), lambda b, i: (0, i, 0)),
            _const_spec(vone.shape),
        ],
        out_specs=[head_spec, head_spec, vt_spec] + dil_specs,
        out_shape=[head_shape, head_shape, vt_shape] + dil_shapes,
        scratch_shapes=[pltpu.VMEM((3 * DIL_WIDTH // LANES, rows, LANES), F32)],
        compiler_params=_params("parallel", "parallel"),
        name="in_projection",
    )(x, w_in_p, gq, gkv, wq2, wk, wvt, tables, vone)
    return outs[0], outs[1], outs[2], [outs[3 + 3 * d:6 + 3 * d] for d in range(len(DIL_PAIRS))]


def _mla_kernel(q_ref, k_ref, vt_ref, o_ref, acc_ref, m_ref, st_ref, pt_ref, *, blk):
    nq = q_ref.shape[2] // blk
    strip = 64
    unroll = 4
    m_ref[...] = jnp.full_like(m_ref, -NEG_BIG)
    acc_ref[...] = jnp.zeros_like(acc_ref)

    def rows(i):
        return pl.ds(i * blk if isinstance(i, int) else pl.multiple_of(i * blk, blk), blk)

    def scores(item, slot):
        qi, j = item
        for hh in range(2):
            st_ref[hh, slot] = _dot_nt(k_ref[0, hh, rows(j), :], q_ref[0, hh, rows(qi), :])

    def update(item, slot, masked):
        qi, j = item

        def band(hh, s):
            tile = st_ref[hh, slot, s * strip:(s + 1) * strip, :]
            if masked:
                key = lax.broadcasted_iota(jnp.int32, (strip, blk), 0) + s * strip
                qry = lax.broadcasted_iota(jnp.int32, (strip, blk), 1)
                tile = jnp.where(key <= qry, tile, -NEG_BIG)
            return tile

        for hh in range(2):
            top = None
            for s in range(blk // strip):
                part = band(hh, s).reshape(strip // 8, 8, blk).max(axis=0)
                top = part if top is None else jnp.maximum(top, part)
            m_old = m_ref[qi, hh]
            m_new = jnp.maximum(m_old, jnp.max(top, axis=0, keepdims=True))
            for s in range(blk // strip):
                pt_ref[hh, s * strip:(s + 1) * strip, :] = jnp.exp2(band(hh, s) - m_new).astype(BF16)
            acc_ref[qi, hh] = acc_ref[qi, hh] * jnp.exp2(m_old - m_new) + _dot(vt_ref[0, hh, j], pt_ref[hh])
            m_ref[qi, hh] = m_new

    def sweep(items, advance, masked):
        n = len(items)
        peel = (n - 1) % unroll
        scores(items[0], 0)
        for t in range(peel):
            scores(items[t + 1], (t + 1) % 2)
            update(items[t], t % 2, masked)

        def body(_, cur):
            for k in range(unroll):
                nxt = advance(cur)
                scores(nxt, (peel + k + 1) % 2)
                update(cur, (peel + k) % 2, masked)
                cur = nxt
            return cur

        start = (jnp.int32(items[peel][0]), jnp.int32(items[peel][1]))
        lax.fori_loop(0, (n - 1) // unroll, body, start)
        update(items[n - 1], (n - 1) % 2, masked)

    def next_below(item):
        qi, j = item
        wrap = j + 1 >= qi
        return jnp.where(wrap, qi + 1, qi), jnp.where(wrap, 0, j + 1)

    sweep([(qi, j) for qi in range(1, nq) for j in range(qi)], next_below, False)
    sweep([(qi, qi) for qi in range(nq)], lambda item: (item[0] + 1, item[1] + 1), True)
    for qi in range(nq):
        outs = []
        for hh in range(2):
            acc = acc_ref[qi, hh]
            outs.append(acc[:HEAD_DIM] / acc[HEAD_DIM:HEAD_DIM + 1])
        o_ref[0, rows(qi), :] = jnp.concatenate(outs, axis=0).T.astype(o_ref.dtype)


def _mla_attention(q, k, vt):
    B, H, S, _ = q.shape
    blk = vt.shape[-1]
    qk_spec = pl.BlockSpec((1, 2, S, LANES), lambda b, hp: (b, hp, 0, 0))
    vt_spec = pl.BlockSpec((1, 2, S // blk, LANES, blk), lambda b, hp: (b, hp, 0, 0, 0))
    return pl.pallas_call(
        functools.partial(_mla_kernel, blk=blk),
        grid=(B, H // 2),
        in_specs=[qk_spec, qk_spec, vt_spec],
        out_specs=pl.BlockSpec((1, S, LANES), lambda b, hp: (b, 0, hp)),
        out_shape=jax.ShapeDtypeStruct((B, S, H * HEAD_DIM), BF16),
        scratch_shapes=[pltpu.VMEM((S // blk, 2, LANES, blk), F32),
                        pltpu.VMEM((S // blk, 2, 1, blk), F32),
                        pltpu.VMEM((2, 2, blk, blk), F32),
                        pltpu.VMEM((2, blk, blk), BF16)],
        compiler_params=_params("parallel", "parallel"),
        name="mla_attention",
    )(q, k, vt)


def _dil_bias_tiles(bias_ref, dil):
    blk = DIL_BLOCK
    key = lax.broadcasted_iota(jnp.int32, (2 * blk, blk), 0)
    qry = lax.broadcasted_iota(jnp.int32, (2 * blk, blk), 1)
    off = qry + blk - key
    steps = jnp.where((off >= 0) & (off <= blk), off.astype(F32), NEG_BIG)
    for pair in range(DIL_HEADS // 2):
        halves = []
        for hh in range(2):
            slope = 2.0 ** (-8.0 * (2 * pair + hh + 1) / DIL_HEADS)
            halves.append(steps * (-slope * dil * math.log2(math.e)))
        bias_ref[pair] = jnp.concatenate(halves, axis=1)


def _dil_kernel(q_ref, kc_ref, kp_ref, vc_ref, vp_ref, o_ref, lse_ref, k_buf, v_buf, st_scr, o_scr, bias_ref, *, dil):
    blk = DIL_BLOCK
    pairs = DIL_HEADS // 2
    nb = q_ref.shape[2] // blk
    units = dil * nb
    _dil_bias_tiles(bias_ref, dil)
    k_buf[:, :blk, :] = kp_ref[0]
    k_buf[:, blk:, :] = kc_ref[0]
    v_buf[:, :blk, :] = vp_ref[0]
    v_buf[:, blk:, :] = vc_ref[0]
    first_span = pl.program_id(1) == 0
    lane = lax.broadcasted_iota(jnp.int32, (blk, LANES), 1)
    first = lane < HEAD_DIM

    def split(u):
        if isinstance(u, int) or nb == 1:
            return u // nb, u % nb if nb > 1 else 0
        return lax.shift_right_logical(u, nb.bit_length() - 1), u & (nb - 1)

    def row0(i):
        return i * blk if isinstance(i, int) else pl.multiple_of(i * blk, blk)

    def scores(u):
        r, i = split(u)
        q = q_ref[0, r, pl.ds(row0(i), blk), :]
        keys = pl.ds(row0(i), 2 * blk)
        tiles = []
        for pair in range(pairs):
            cols = slice(pair * LANES, (pair + 1) * LANES)
            qp = q[:, cols]
            zero = jnp.zeros_like(qp)
            q2 = jnp.concatenate([jnp.where(first, qp, zero), jnp.where(first, zero, qp)], axis=0)
            tiles.append(_dot_nt(k_buf[r, keys, cols], q2))
        return tiles

    def finish(u, tiles):
        r, i = split(u)
        keys = pl.ds(row0(i), 2 * blk)
        pen = jnp.where(jnp.logical_and(first_span, i == 0), -NEG_BIG, 0.0)
        if dil == 1:
            rows = pl.ds(row0(i), blk)
        else:
            rows = pl.ds(r + i * (blk * dil), blk, stride=dil)
        for pair in range(pairs):
            cols = slice(pair * LANES, (pair + 1) * LANES)
            st = tiles[pair] + bias_ref[pair]
            st = jnp.concatenate([st[:blk] + pen, st[blk:]], axis=0)
            m = jnp.max(st, axis=0, keepdims=True)
            pt = jnp.exp2(st - m)
            l = jnp.sum(pt, axis=0, keepdims=True)
            res = lax.dot_general(v_buf[r, keys, cols], pt.astype(BF16), (((0,), (0,)), ((), ())),
                                  preferred_element_type=F32)
            inv = 1.0 / l
            ot = jnp.concatenate([res[:HEAD_DIM, :LANES] * inv[:, :LANES],
                                  res[HEAD_DIM:, LANES:] * inv[:, LANES:]], axis=0)
            o_scr[pair, rows, :] = ot.T
            lse2 = m + jnp.log2(l)
            lse_ref[0, r, i, 2 * pair:2 * pair + 1, :] = lse2[:, :LANES]
            lse_ref[0, r, i, 2 * pair + 1:2 * pair + 2, :] = lse2[:, LANES:]

    for pair, tile in enumerate(scores(0)):
        st_scr[pair] = tile

    def body(u, carry):
        tiles = [st_scr[pair] for pair in range(pairs)]
        ahead = scores(u + 1)
        finish(u, tiles)
        for pair, tile in enumerate(ahead):
            st_scr[pair] = tile
        return carry

    lax.fori_loop(0, units - 1, body, 0)
    finish(units - 1, [st_scr[pair] for pair in range(pairs)])
    for pair in range(pairs):
        o_ref[0, :, pair * LANES:(pair + 1) * LANES] = o_scr[pair].astype(o_ref.dtype)


def _dil_branch(q, k, v, dil):
    B, _, M, W = q.shape
    S = M * dil
    span = min(DIL_SPAN, S)
    assert span % (dil * DIL_BLOCK) == 0 and S % span == 0
    rows = span // dil
    nb = rows // DIL_BLOCK
    cur = pl.BlockSpec((1, dil, rows, W), lambda b, c: (b, 0, c, 0))
    prev = pl.BlockSpec((1, dil, DIL_BLOCK, W), lambda b, c: (b, 0, jnp.maximum(c * nb - 1, 0), 0))
    o, lse = pl.pallas_call(
        functools.partial(_dil_kernel, dil=dil),
        grid=(B, S // span),
        in_specs=[cur, cur, prev, cur, prev],
        out_specs=[pl.BlockSpec((1, span, W), lambda b, c: (b, c, 0)),
                   pl.BlockSpec((1, dil, nb, DIL_HEADS, LANES), lambda b, c: (b, 0, c, 0, 0))],
        out_shape=[jax.ShapeDtypeStruct((B, S, W), BF16),
                   jax.ShapeDtypeStruct((B, dil, M // DIL_BLOCK, DIL_HEADS, LANES), F32)],
        scratch_shapes=[pltpu.VMEM((dil, DIL_BLOCK + rows, W), BF16),
                        pltpu.VMEM((dil, DIL_BLOCK + rows, W), BF16),
                        pltpu.VMEM((DIL_HEADS // 2, 2 * DIL_BLOCK, 2 * LANES), F32),
                        pltpu.VMEM((DIL_HEADS // 2, span, LANES), F32),
                        pltpu.VMEM((DIL_HEADS // 2, 2 * DIL_BLOCK, 2 * LANES), F32)],
        compiler_params=_params("parallel", "arbitrary"),
        name=f"dilated_d{dil}",
    )(q, k, k, v, v)
    return o, lse.transpose(0, 2, 4, 1, 3).reshape(B, S, DIL_HEADS)


def _dilated_attention(qkv_per_dil):
    outs, lses = [], []
    for (window, dil), (q, k, v) in zip(DIL_PAIRS, qkv_per_dil):
        assert window // dil == DIL_BLOCK
        o, lse = _dil_branch(q, k, v, dil)
        outs.append(o)
        lses.append(lse)
    lse_all = jnp.concatenate(lses, axis=-1)
    return outs, jnp.pad(lse_all, ((0, 0), (0, 0), (0, LANES - lse_all.shape[-1])))


def _layer_norm(y, g, b):
    mu = jnp.mean(y, axis=-1, keepdims=True)
    d = y - mu
    var = jnp.mean(d * d, axis=-1, keepdims=True)
    return d * lax.rsqrt(var + LN_EPS) * g + b


def _out_proj_kernel(om_ref, o0_ref, o1_ref, o2_ref, lse_ref, x_ref, wo_ref, ex_ref, g_ref, b_ref, o_ref):
    H, W = DIL_HEADS, DIL_WIDTH
    l0 = lse_ref[0]
    l1 = pltpu.roll(l0, LANES - H, 1)
    l2 = pltpu.roll(l0, LANES - 2 * H, 1)
    top = jnp.maximum(l0, jnp.maximum(l1, l2))
    e0, e1, e2 = jnp.exp2(l0 - top), jnp.exp2(l1 - top), jnp.exp2(l2 - top)
    inv = 1.0 / (e0 + e1 + e2)
    lane = lax.broadcasted_iota(jnp.int32, l0.shape, 1)
    w = jnp.where(lane < H, e0 * inv,
                  jnp.where(lane < 2 * H, pltpu.roll(e1 * inv, H, 1), pltpu.roll(e2 * inv, 2 * H, 1)))
    wide = _dot(w.astype(BF16), ex_ref[...])
    o_dil = (wide[:, :W] * o0_ref[0] + wide[:, W:2 * W] * o1_ref[0] + wide[:, 2 * W:] * o2_ref[0]).astype(BF16)
    half = om_ref.shape[-1]
    mix = _dot(om_ref[0], wo_ref[:half, :]) + _dot(o_dil, wo_ref[half:, :])
    o_ref[0] = _layer_norm(DN_ALPHA * x_ref[0] + mix, g_ref[...], b_ref[...])


def _out_projection(o_mla, o_dils, lse_all, x, w_o, expand, g, b):
    B, S, _ = x.shape
    rows = min(PROJ_ROWS, S)
    half_spec = pl.BlockSpec((1, rows, o_mla.shape[-1]), lambda bi, i: (bi, i, 0))
    lse_spec = pl.BlockSpec((1, rows, LANES), lambda bi, i: (bi, i, 0))
    x_spec = pl.BlockSpec((1, rows, D_MODEL), lambda bi, i: (bi, i, 0))
    return pl.pallas_call(
        _out_proj_kernel,
        grid=(B, S // rows),
        in_specs=[half_spec, half_spec, half_spec, half_spec, lse_spec, x_spec, _const_spec(w_o.shape),
                  _const_spec(expand.shape), _const_spec(g.shape), _const_spec(b.shape)],
        out_specs=x_spec,
        out_shape=jax.ShapeDtypeStruct(x.shape, F32),
        compiler_params=_params("parallel", "parallel"),
        name="out_projection_ln",
    )(o_mla, *o_dils, lse_all, x, w_o, expand, g, b)


def _ffn_kernel(x_ref, halo_ref, wa_ref, wg_ref, cw_ref, wd_ref, g_ref, b_ref, o_ref, acc_ref, u0_ref, u1_ref):
    rows = x_ref.shape[1]
    pad = halo_ref.shape[1]
    chunks = wa_ref.shape[0]
    x = x_ref[0]
    halo = jnp.where(pl.program_id(1) == 0, 0.0, halo_ref[0])
    xe = jnp.concatenate([halo, x], axis=0).astype(BF16)
    acc_ref[...] = jnp.zeros_like(acc_ref)

    def up(c, u_ref):
        u_ref[0] = _dot(xe, wa_ref[c])
        u_ref[1] = _dot(xe, wg_ref[c])

    def conv(u_ref, part, taps):
        y = taps[3:4, :]
        for j in range(CONV_WIDTH):
            shift = CONV_WIDTH - 1 - j
            y = y + taps[j:j + 1, :] * u_ref[part, pl.ds(pad - shift, rows), :]
        return y

    def down(c, u_ref):
        taps = cw_ref[c]
        ya = conv(u_ref, 0, taps[0:4])
        yg = conv(u_ref, 1, taps[4:8])
        hidden = (jax.nn.gelu(yg) * ya).astype(BF16)
        acc_ref[...] += _dot(hidden, wd_ref[c])

    assert chunks % 2 == 1
    up(0, u0_ref)

    def body(j, carry):
        c = 2 * j
        up(c + 1, u1_ref)
        down(c, u0_ref)
        up(c + 2, u0_ref)
        down(c + 1, u1_ref)
        return carry

    lax.fori_loop(0, chunks // 2, body, 0)
    down(chunks - 1, u0_ref)
    o_ref[0] = _layer_norm(DN_ALPHA * x + acc_ref[...], g_ref[...], b_ref[...])


def _ffn(x1, wa, wg, cw, wd, g, b):
    B, S, _ = x1.shape
    rows = min(FFN_ROWS, S)
    pad = BF16_ROWS
    x_spec = pl.BlockSpec((1, rows, D_MODEL), lambda bi, i: (bi, i, 0))
    halo_spec = pl.BlockSpec((1, pad, D_MODEL), lambda bi, i: (bi, jnp.maximum(i * (rows // pad) - 1, 0), 0))
    return pl.pallas_call(
        _ffn_kernel,
        grid=(B, S // rows),
        in_specs=[x_spec, halo_spec, _const_spec(wa.shape), _const_spec(wg.shape), _const_spec(cw.shape),
                  _const_spec(wd.shape), _const_spec(g.shape), _const_spec(b.shape)],
        out_specs=x_spec,
        out_shape=jax.ShapeDtypeStruct(x1.shape, F32),
        scratch_shapes=[pltpu.VMEM((rows, D_MODEL), F32),
                        pltpu.VMEM((2, pad + rows, FFN_CHUNK), F32), pltpu.VMEM((2, pad + rows, FFN_CHUNK), F32)],
        compiler_params=_params("parallel", "parallel"),
        name="conv_ffn_ln",
    )(x1, x1, wa, wg, cw, wd, g, b)


def _pad_cols(w, width):
    return jnp.pad(w, ((0, 0), (0, width - w.shape[1])))


def _head_groups(w):
    rank, heads, e = w.shape
    return jnp.pad(w, ((0, 0), (0, 0), (0, LANES - e))).reshape(rank, heads * LANES)


def _prepare(w_in, g_cq, g_ckv, w_uq, w_uk, w_uv, w_o, ln1_g, ln1_b, w_up, conv_w, conv_b, w_down, ln2_g, ln2_b):
    r0, r1, r2 = MLA_Q_RANK, MLA_Q_RANK + MLA_KV_RANK, MLA_Q_RANK + MLA_KV_RANK + MLA_ROPE_DIM
    w_in_p = jnp.concatenate([w_in[:, :r1], _pad_cols(w_in[:, r1:r2], LANES), w_in[:, r2:]], axis=1).astype(BF16)

    half = MLA_ROPE_DIM // 2
    rope = w_uq[:, :, MLA_NOPE_DIM:]
    swapped = jnp.concatenate([-rope[:, :, half:], rope[:, :, :half]], axis=-1)
    swapped = jnp.concatenate([jnp.zeros_like(w_uq[:, :, :MLA_NOPE_DIM]), swapped], axis=-1)
    wq2 = jnp.concatenate([_head_groups(w_uq), _head_groups(swapped)], axis=1).astype(BF16)

    wk = _head_groups(w_uk).astype(BF16)
    wvt = jnp.pad(w_uv.transpose(1, 2, 0), ((0, 0), (0, LANES - HEAD_DIM), (0, 0))).astype(BF16)
    vone = jnp.zeros((LANES, 1), F32).at[HEAD_DIM, 0].set(1.0)

    n_chunks = D_FF // FFN_CHUNK
    wa = w_up[:, :D_FF].reshape(D_MODEL, n_chunks, FFN_CHUNK).transpose(1, 0, 2).astype(BF16)
    wg = w_up[:, D_FF:].reshape(D_MODEL, n_chunks, FFN_CHUNK).transpose(1, 0, 2).astype(BF16)
    taps = jnp.concatenate([conv_w, conv_b[None, :]], axis=0)
    cw = jnp.concatenate([taps[:, :D_FF].reshape(4, n_chunks, FFN_CHUNK),
                          taps[:, D_FF:].reshape(4, n_chunks, FFN_CHUNK)], axis=0).transpose(1, 0, 2)
    wd = w_down.reshape(n_chunks, FFN_CHUNK, D_MODEL).astype(BF16)
    src = jnp.arange(LANES)[:, None]
    dst = jnp.arange(len(DIL_PAIRS) * DIL_WIDTH)[None, :]
    expand = (src == dst // HEAD_DIM).astype(BF16)
    row = lambda a: a.reshape(1, -1)
    return dict(w_in_p=w_in_p, gq=row(g_cq), gkv=row(g_ckv), wq2=wq2, wk=wk, wvt=wvt, vone=vone,
                w_o=w_o.astype(BF16), expand=expand, ln1=(row(ln1_g), row(ln1_b)), wa=wa, wg=wg, cw=cw, wd=wd,
                ln2=(row(ln2_g), row(ln2_b)))


def _freq_lanes():
    half = MLA_ROPE_DIM // 2
    freqs = ROPE_THETA ** (-jnp.arange(half, dtype=F32) / half)
    zeros = jnp.zeros((MLA_NOPE_DIM,), F32)
    return jnp.concatenate([zeros, freqs, freqs, jnp.zeros((LANES - MLA_NOPE_DIM - MLA_ROPE_DIM,), F32)])[None, :]


def kernel(x, w_in, g_cq, g_ckv, w_uq, w_uk, w_uv, w_o, ln1_g, ln1_b, w_up, conv_w, conv_b, w_down, ln2_g, ln2_b):
    B, S, _ = x.shape
    p = _prepare(w_in, g_cq, g_ckv, w_uq, w_uk, w_uv, w_o, ln1_g, ln1_b, w_up, conv_w, conv_b, w_down, ln2_g, ln2_b)
    scale = math.log2(math.e) / math.sqrt(MLA_NOPE_DIM + MLA_ROPE_DIM)
    tables = _rope_tables(S, _freq_lanes(), scale)
    q, k, vt, qkv_dil = _projection(x, p["w_in_p"], p["gq"], p["gkv"], p["wq2"], p["wk"], p["wvt"], tables,
                                    p["vone"])
    o_mla = _mla_attention(q, k, vt)
    o_dils, lse_all = _dilated_attention(qkv_dil)
    x1 = _out_projection(o_mla, o_dils, lse_all, x, p["w_o"], p["expand"], *p["ln1"])
    return _ffn(x1, p["wa"], p["wg"], p["cw"], p["wd"], *p["ln2"])
```

```python
import functools
import math

import jax
import jax.numpy as jnp
from jax import lax
from jax.experimental import pallas as pl
from jax.experimental.pallas import tpu as pltpu

D_MODEL = 1024
HEAD_DIM = 64
MLA_HEADS = 8
MLA_Q_RANK = 256
MLA_KV_RANK = 128
MLA_NOPE_DIM = 64
MLA_ROPE_DIM = 32
ROPE_THETA = 10000.0
DIL_HEADS = 8
DIL_PAIRS = ((128, 1), (512, 4), (2048, 16))
DIL_BLOCK = 128
DIL_WIDTH = DIL_HEADS * HEAD_DIM
D_FF = 2816
CONV_WIDTH = 3
DEPTH = 1
DN_ALPHA = (2.0 * DEPTH) ** 0.25
LN_EPS = 1e-5
RMS_EPS = 1e-6

LANES = 128
BF16_ROWS = 16
VMEM_LIMIT = 56 * 1024 * 1024

PROJ_ROWS = 512
DIL_SPAN = 2048
FFN_ROWS = 512
FFN_CHUNK = 256
NEG_BIG = 1e30

BF16 = jnp.bfloat16
F32 = jnp.float32


def _dot(a, b):
    return jnp.dot(a, b, preferred_element_type=F32)


def _dot_nt(a, b):
    return lax.dot_general(a, b, (((1,), (1,)), ((), ())), preferred_element_type=F32)


def _params(*sem, flags=None):
    return pltpu.CompilerParams(dimension_semantics=sem, vmem_limit_bytes=VMEM_LIMIT, flags=flags)


def _const_spec(shape):
    zeros = (0,) * len(shape)
    return pl.BlockSpec(shape, lambda *_: zeros, pipeline_mode=pl.Buffered(1))


def _rope_table_kernel(freq_ref, out_ref, *, scale):
    rows = out_ref.shape[1]
    pos = lax.broadcasted_iota(jnp.int32, (rows, LANES), 0).astype(F32)
    ang = pos * freq_ref[...]
    c = jnp.cos(ang)
    s = jnp.sin(ang)
    out_ref[0] = c
    out_ref[1] = s
    out_ref[2] = c * scale
    out_ref[3] = s * scale


def _rope_tables(seq, freq_lanes, scale):
    return pl.pallas_call(
        functools.partial(_rope_table_kernel, scale=scale),
        out_shape=jax.ShapeDtypeStruct((4, seq, LANES), F32),
        name="rope_tables",
    )(freq_lanes)


def _rms(x, g):
    ms = jnp.mean(x * x, axis=-1, keepdims=True)
    return x * lax.rsqrt(ms + RMS_EPS) * g


def _proj_kernel(x_ref, w_in_ref, gq_ref, gkv_ref, wq_ref, wk_ref, wvt_ref, tab_ref, vone_ref,
                 q_ref, k_ref, vt_ref, *rest):
    dil_refs, hd_ref = rest[:-1], rest[-1]
    rows = x_ref.shape[1]
    xb = x_ref[0].astype(BF16)
    h = _dot(xb, w_in_ref[...])
    cq = _rms(h[:, :MLA_Q_RANK], gq_ref[...]).astype(BF16)
    ckv = _rms(h[:, MLA_Q_RANK:MLA_Q_RANK + MLA_KV_RANK], gkv_ref[...]).astype(BF16)
    kr = h[:, 3 * LANES:4 * LANES]
    cos, sin, cos_q, sin_q = tab_ref[0], tab_ref[1], tab_ref[2], tab_ref[3]

    k_plain = pltpu.roll(kr, 64, 1)
    k_swap = pltpu.roll(kr, 80, 1) - pltpu.roll(kr, 48, 1)
    k_rope = k_plain * cos + k_swap * sin

    q2 = _dot(cq, wq_ref[...])
    kn = _dot(ckv, wk_ref[...])
    hw = MLA_HEADS * LANES
    for hd in range(MLA_HEADS):
        lo = hd * LANES
        qh = q2[:, lo:lo + LANES] * cos_q + q2[:, hw + lo:hw + lo + LANES] * sin_q
        q_ref[0, hd] = qh.astype(BF16)
        k_ref[0, hd] = (kn[:, lo:lo + LANES] + k_rope).astype(BF16)
        vt_ref[0, hd, 0] = (_dot_nt(wvt_ref[hd], ckv) + vone_ref[...]).astype(BF16)

    base = 4 * LANES
    groups = DIL_WIDTH // LANES
    for c in range(3 * groups):
        slab = h[:, base + c * LANES:base + (c + 1) * LANES]
        hd_ref[c] = slab * (math.log2(math.e) / math.sqrt(HEAD_DIM)) if c < groups else slab
    for di, (_, dil) in enumerate(DIL_PAIRS):
        for part in range(3):
            out = dil_refs[3 * di + part]
            for r in range(dil):
                for g in range(groups):
                    piece = hd_ref[part * groups + g, pl.ds(r, rows // dil, stride=dil), :]
                    out[0, r, :, g * LANES:(g + 1) * LANES] = piece.astype(BF16)


def _projection(x, w_in_p, gq, gkv, wq2, wk, wvt, tables, vone):
    B, S, _ = x.shape
    rows = min(PROJ_ROWS, S)
    head_shape = jax.ShapeDtypeStruct((B, MLA_HEADS, S, LANES), BF16)
    vt_shape = jax.ShapeDtypeStruct((B, MLA_HEADS, S // rows, LANES, rows), BF16)
    head_spec = pl.BlockSpec((1, MLA_HEADS, rows, LANES), lambda b, i: (b, 0, i, 0))
    vt_spec = pl.BlockSpec((1, MLA_HEADS, 1, LANES, rows), lambda b, i: (b, 0, i, 0, 0))
    dil_shapes, dil_specs = [], []
    for _, dil in DIL_PAIRS:
        assert rows % (dil * BF16_ROWS) == 0
        dil_shapes += [jax.ShapeDtypeStruct((B, dil, S // dil, DIL_WIDTH), BF16)] * 3
        dil_specs += [pl.BlockSpec((1, dil, rows // dil, DIL_WIDTH), lambda b, i: (b, 0, i, 0))] * 3
    outs = pl.pallas_call(
        _proj_kernel,
        grid=(B, S // rows),
        in_specs=[
            pl.BlockSpec((1, rows, D_MODEL), lambda b, i: (b, i, 0)),
            _const_spec(w_in_p.shape),
            _const_spec(gq.shape),
            _const_spec(gkv.shape),
            _const_spec(wq2.shape),
            _const_spec(wk.shape),
            _const_spec(wvt.shape),
            pl.BlockSpec((4, rows, LANES), lambda b, i: (0, i, 0)),
            _const_spec(vone.shape),
        ],
        out_specs=[head_spec, head_spec, vt_spec] + dil_specs,
        out_shape=[head_shape, head_shape, vt_shape] + dil_shapes,
        scratch_shapes=[pltpu.VMEM((3 * DIL_WIDTH // LANES, rows, LANES), F32)],
        compiler_params=_params("parallel", "parallel"),
        name="in_projection",
    )(x, w_in_p, gq, gkv, wq2, wk, wvt, tables, vone)
    return outs[0], outs[1], outs[2], [outs[3 + 3 * d:6 + 3 * d] for d in range(len(DIL_PAIRS))]


def _mla_kernel(q_ref, k_ref, vt_ref, o_ref, acc_ref, m_ref, st_ref, pt_ref, *, blk):
    nq = q_ref.shape[2] // blk
    strip = 64
    unroll = 4
    m_ref[...] = jnp.full_like(m_ref, -NEG_BIG)
    acc_ref[...] = jnp.zeros_like(acc_ref)

    def rows(i):
        return pl.ds(i * blk if isinstance(i, int) else pl.multiple_of(i * blk, blk), blk)

    def scores(item, slot):
        qi, j = item
        for hh in range(2):
            st_ref[hh, slot] = _dot_nt(k_ref[0, hh, rows(j), :], q_ref[0, hh, rows(qi), :])

    def update(item, slot, masked):
        qi, j = item

        def band(hh, s):
            tile = st_ref[hh, slot, s * strip:(s + 1) * strip, :]
            if masked:
                key = lax.broadcasted_iota(jnp.int32, (strip, blk), 0) + s * strip
                qry = lax.broadcasted_iota(jnp.int32, (strip, blk), 1)
                tile = jnp.where(key <= qry, tile, -NEG_BIG)
            return tile

        for hh in range(2):
            top = None
            for s in range(blk // strip):
                part = band(hh, s).reshape(strip // 8, 8, blk).max(axis=0)
                top = part if top is None else jnp.maximum(top, part)
            m_old = m_ref[qi, hh]
            m_new = jnp.maximum(m_old, jnp.max(top, axis=0, keepdims=True))
            for s in range(blk // strip):
                pt_ref[hh, s * strip:(s + 1) * strip, :] = jnp.exp2(band(hh, s) - m_new).astype(BF16)
            acc_ref[qi, hh] = acc_ref[qi, hh] * jnp.exp2(m_old - m_new) + _dot(vt_ref[0, hh, j], pt_ref[hh])
            m_ref[qi, hh] = m_new

    def sweep(items, advance, masked):
        n = len(items)
        peel = (n - 1) % unroll
        scores(items[0], 0)
        for t in range(peel):
            scores(items[t + 1], (t + 1) % 2)
            update(items[t], t % 2, masked)

        def body(_, cur):
            for k in range(unroll):
                nxt = advance(cur)
                scores(nxt, (peel + k + 1) % 2)
                update(cur, (peel + k) % 2, masked)
                cur = nxt
            return cur

        start = (jnp.int32(items[peel][0]), jnp.int32(items[peel][1]))
        lax.fori_loop(0, (n - 1) // unroll, body, start)
        update(items[n - 1], (n - 1) % 2, masked)

    def next_below(item):
        qi, j = item
        wrap = j + 1 >= qi
        return jnp.where(wrap, qi + 1, qi), jnp.where(wrap, 0, j + 1)

    sweep([(qi, j) for qi in range(1, nq) for j in range(qi)], next_below, False)
    sweep([(qi, qi) for qi in range(nq)], lambda item: (item[0] + 1, item[1] + 1), True)
    for qi in range(nq):
        outs = []
        for hh in range(2):
            acc = acc_ref[qi, hh]
            outs.append(acc[:HEAD_DIM] / acc[HEAD_DIM:HEAD_DIM + 1])
        o_ref[0, rows(qi), :] = jnp.concatenate(outs, axis=0).T.astype(o_ref.dtype)


def _mla_attention(q, k, vt):
    B, H, S, _ = q.shape
    blk = vt.shape[-1]
    qk_spec = pl.BlockSpec((1, 2, S, LANES), lambda b, hp: (b, hp, 0, 0))
    vt_spec = pl.BlockSpec((1, 2, S // blk, LANES, blk), lambda b, hp: (b, hp, 0, 0, 0))
    return pl.pallas_call(
        functools.partial(_mla_kernel, blk=blk),
        grid=(B, H // 2),
        in_specs=[qk_spec, qk_spec, vt_spec],
        out_specs=pl.BlockSpec((1, S, LANES), lambda b, hp: (b, 0, hp)),
        out_shape=jax.ShapeDtypeStruct((B, S, H * HEAD_DIM), BF16),
        scratch_shapes=[pltpu.VMEM((S // blk, 2, LANES, blk), F32),
                        pltpu.VMEM((S // blk, 2, 1, blk), F32),
                        pltpu.VMEM((2, 2, blk, blk), F32),
                        pltpu.VMEM((2, blk, blk), BF16)],
        compiler_params=_params("parallel", "parallel"),
        name="mla_attention",
    )(q, k, vt)


def _dil_bias_tiles(bias_ref, dil):
    blk = DIL_BLOCK
    key = lax.broadcasted_iota(jnp.int32, (2 * blk, blk), 0)
    qry = lax.broadcasted_iota(jnp.int32, (2 * blk, blk), 1)
    off = qry + blk - key
    steps = jnp.where((off >= 0) & (off <= blk), off.astype(F32), NEG_BIG)
    for pair in range(DIL_HEADS // 2):
        halves = []
        for hh in range(2):
            slope = 2.0 ** (-8.0 * (2 * pair + hh + 1) / DIL_HEADS)
            halves.append(steps * (-slope * dil * math.log2(math.e)))
        bias_ref[pair] = jnp.concatenate(halves, axis=1)


def _dil_needs_copy(dil, nb):
    return not (nb == 1 or dil == 1)


def _dil_kernel(q_ref, kc_ref, kp_ref, vc_ref, vp_ref, o_ref, lse_ref, st_scr, o_scr, bias_ref, *bufs, dil):
    blk = DIL_BLOCK
    pairs = DIL_HEADS // 2
    nb = q_ref.shape[2] // blk
    units = dil * nb
    _dil_bias_tiles(bias_ref, dil)
    if bufs:
        for buf, prev_ref, cur_ref in zip(bufs, (kp_ref, vp_ref), (kc_ref, vc_ref)):
            buf[:, :blk, :] = prev_ref[0]
            buf[:, blk:, :] = cur_ref[0]
    first_span = pl.program_id(1) == 0
    lane = lax.broadcasted_iota(jnp.int32, (blk, LANES), 1)
    first = lane < HEAD_DIM

    def split(u):
        if nb == 1:
            return u, 0
        if isinstance(u, int):
            return u // nb, u % nb
        return lax.shift_right_logical(u, nb.bit_length() - 1), u & (nb - 1)

    def row0(i):
        return i * blk if isinstance(i, int) else pl.multiple_of(i * blk, blk)

    def window(which, r, i, cols):
        cur_ref, prev_ref = ((kc_ref, kp_ref), (vc_ref, vp_ref))[which]
        if bufs:
            return bufs[which][r, pl.ds(row0(i), 2 * blk), cols]
        if isinstance(i, int) and i == 0:
            return jnp.concatenate([prev_ref[0, r, :, cols], cur_ref[0, r, 0:blk, cols]], axis=0)
        return cur_ref[0, r, pl.ds(row0(i - 1), 2 * blk), cols]

    def scores(u, slot):
        r, i = split(u)
        q = q_ref[0, r, pl.ds(row0(i), blk), :]
        for pair in range(pairs):
            cols = slice(pair * LANES, (pair + 1) * LANES)
            qp = q[:, cols]
            zero = jnp.zeros_like(qp)
            q2 = jnp.concatenate([jnp.where(first, qp, zero), jnp.where(first, zero, qp)], axis=0)
            st_scr[slot, pair] = _dot_nt(window(0, r, i, cols), q2) + bias_ref[pair]

    def finish(u, slot):
        r, i = split(u)
        pen = jnp.where(jnp.logical_and(first_span, i == 0), -NEG_BIG, 0.0)
        if dil == 1:
            rows = pl.ds(row0(i), blk)
        else:
            rows = pl.ds(r + i * (blk * dil), blk, stride=dil)
        for pair in range(pairs):
            cols = slice(pair * LANES, (pair + 1) * LANES)
            st = jnp.concatenate([st_scr[slot, pair, :blk, :] + pen, st_scr[slot, pair, blk:, :]], axis=0)
            m = jnp.max(st, axis=0, keepdims=True)
            pt = jnp.exp2(st - m)
            l = jnp.sum(pt, axis=0, keepdims=True)
            res = lax.dot_general(window(1, r, i, cols), pt.astype(BF16), (((0,), (0,)), ((), ())),
                                  preferred_element_type=F32)
            inv = 1.0 / l
            ot = jnp.concatenate([res[:HEAD_DIM, :LANES] * inv[:, :LANES],
                                  res[HEAD_DIM:, LANES:] * inv[:, LANES:]], axis=0)
            o_scr[pair, rows, :] = ot.T
            lse2 = m + jnp.log2(l)
            lse_ref[0, r, i, 2 * pair:2 * pair + 1, :] = lse2[:, :LANES]
            lse_ref[0, r, i, 2 * pair + 1:2 * pair + 2, :] = lse2[:, LANES:]

    assert units % 2 == 0
    scores(0, 0)
    scores(1, 1)
    finish(0, 0)

    def body(j, carry):
        u = 2 * j + 1
        scores(u + 1, 0)
        finish(u, 1)
        scores(u + 2, 1)
        finish(u + 1, 0)
        return carry

    lax.fori_loop(0, (units - 2) // 2, body, 0)
    finish(units - 1, 1)
    for pair in range(pairs):
        o_ref[0, :, pair * LANES:(pair + 1) * LANES] = o_scr[pair].astype(o_ref.dtype)


def _dil_branch(q, k, v, dil):
    B, _, M, W = q.shape
    S = M * dil
    span = min(DIL_SPAN, S)
    assert span % (dil * DIL_BLOCK) == 0 and S % span == 0
    rows = span // dil
    nb = rows // DIL_BLOCK
    cur = pl.BlockSpec((1, dil, rows, W), lambda b, c: (b, 0, c, 0))
    prev = pl.BlockSpec((1, dil, DIL_BLOCK, W), lambda b, c: (b, 0, jnp.maximum(c * nb - 1, 0), 0))
    o, lse = pl.pallas_call(
        functools.partial(_dil_kernel, dil=dil),
        grid=(B, S // span),
        in_specs=[cur, cur, prev, cur, prev],
        out_specs=[pl.BlockSpec((1, span, W), lambda b, c: (b, c, 0)),
                   pl.BlockSpec((1, dil, nb, DIL_HEADS, LANES), lambda b, c: (b, 0, c, 0, 0))],
        out_shape=[jax.ShapeDtypeStruct((B, S, W), BF16),
                   jax.ShapeDtypeStruct((B, dil, M // DIL_BLOCK, DIL_HEADS, LANES), F32)],
        scratch_shapes=[pltpu.VMEM((2, DIL_HEADS // 2, 2 * DIL_BLOCK, 2 * LANES), F32),
                        pltpu.VMEM((DIL_HEADS // 2, span, LANES), F32),
                        pltpu.VMEM((DIL_HEADS // 2, 2 * DIL_BLOCK, 2 * LANES), F32)]
        + [pltpu.VMEM((dil, DIL_BLOCK + rows, W), BF16)] * (2 if _dil_needs_copy(dil, nb) else 0),
        compiler_params=_params("parallel", "arbitrary"),
        name=f"dilated_d{dil}",
    )(q, k, k, v, v)
    return o, lse.transpose(0, 2, 4, 1, 3).reshape(B, S, DIL_HEADS)


def _dilated_attention(qkv_per_dil):
    outs, lses = [], []
    for (window, dil), (q, k, v) in zip(DIL_PAIRS, qkv_per_dil):
        assert window // dil == DIL_BLOCK
        o, lse = _dil_branch(q, k, v, dil)
        outs.append(o)
        lses.append(lse)
    lse_all = jnp.concatenate(lses, axis=-1)
    return outs, jnp.pad(lse_all, ((0, 0), (0, 0), (0, LANES - lse_all.shape[-1])))


def _layer_norm(y, g, b):
    mu = jnp.mean(y, axis=-1, keepdims=True)
    d = y - mu
    var = jnp.mean(d * d, axis=-1, keepdims=True)
    return d * lax.rsqrt(var + LN_EPS) * g + b


def _out_proj_kernel(om_ref, o0_ref, o1_ref, o2_ref, lse_ref, x_ref, wo_ref, ex_ref, g_ref, b_ref, o_ref):
    H, W = DIL_HEADS, DIL_WIDTH
    l0 = lse_ref[0]
    l1 = pltpu.roll(l0, LANES - H, 1)
    l2 = pltpu.roll(l0, LANES - 2 * H, 1)
    top = jnp.maximum(l0, jnp.maximum(l1, l2))
    e0, e1, e2 = jnp.exp2(l0 - top), jnp.exp2(l1 - top), jnp.exp2(l2 - top)
    inv = 1.0 / (e0 + e1 + e2)
    lane = lax.broadcasted_iota(jnp.int32, l0.shape, 1)
    w = jnp.where(lane < H, e0 * inv,
                  jnp.where(lane < 2 * H, pltpu.roll(e1 * inv, H, 1), pltpu.roll(e2 * inv, 2 * H, 1)))
    wide = _dot(w.astype(BF16), ex_ref[...])
    o_dil = (wide[:, :W] * o0_ref[0] + wide[:, W:2 * W] * o1_ref[0] + wide[:, 2 * W:] * o2_ref[0]).astype(BF16)
    half = om_ref.shape[-1]
    mix = _dot(om_ref[0], wo_ref[:half, :]) + _dot(o_dil, wo_ref[half:, :])
    o_ref[0] = _layer_norm(DN_ALPHA * x_ref[0] + mix, g_ref[...], b_ref[...])


def _out_projection(o_mla, o_dils, lse_all, x, w_o, expand, g, b):
    B, S, _ = x.shape
    rows = min(PROJ_ROWS, S)
    half_spec = pl.BlockSpec((1, rows, o_mla.shape[-1]), lambda bi, i: (bi, i, 0))
    lse_spec = pl.BlockSpec((1, rows, LANES), lambda bi, i: (bi, i, 0))
    x_spec = pl.BlockSpec((1, rows, D_MODEL), lambda bi, i: (bi, i, 0))
    return pl.pallas_call(
        _out_proj_kernel,
        grid=(B, S // rows),
        in_specs=[half_spec, half_spec, half_spec, half_spec, lse_spec, x_spec, _const_spec(w_o.shape),
                  _const_spec(expand.shape), _const_spec(g.shape), _const_spec(b.shape)],
        out_specs=x_spec,
        out_shape=jax.ShapeDtypeStruct(x.shape, F32),
        compiler_params=_params("parallel", "parallel"),
        name="out_projection_ln",
    )(o_mla, *o_dils, lse_all, x, w_o, expand, g, b)


def _ffn_kernel(x_ref, halo_ref, wa_ref, wg_ref, cw_ref, wd_ref, g_ref, b_ref, o_ref, acc_ref, u0_ref, u1_ref):
    rows = x_ref.shape[1]
    pad = halo_ref.shape[1]
    chunks = wa_ref.shape[0]
    x = x_ref[0]
    halo = jnp.where(pl.program_id(1) == 0, 0.0, halo_ref[0])
    xe = jnp.concatenate([halo, x], axis=0).astype(BF16)
    acc_ref[...] = jnp.zeros_like(acc_ref)

    def up(c, u_ref):
        u_ref[0] = _dot(xe, wa_ref[c])
        u_ref[1] = _dot(xe, wg_ref[c])

    def conv(u_ref, part, taps):
        y = taps[3:4, :]
        for j in range(CONV_WIDTH):
            shift = CONV_WIDTH - 1 - j
            y = y + taps[j:j + 1, :] * u_ref[part, pl.ds(pad - shift, rows), :]
        return y

    def down(c, u_ref):
        taps = cw_ref[c]
        ya = conv(u_ref, 0, taps[0:4])
        yg = conv(u_ref, 1, taps[4:8])
        c1 = math.sqrt(2.0 / math.pi)
        th = jnp.tanh(yg * (c1 + (c1 * 0.044715) * (yg * yg)))
        hidden = ((yg + yg * th) * ya).astype(BF16)
        acc_ref[...] += _dot(hidden, wd_ref[c])

    assert chunks % 2 == 1
    up(0, u0_ref)

    def body(j, carry):
        c = 2 * j
        up(c + 1, u1_ref)
        down(c, u0_ref)
        up(c + 2, u0_ref)
        down(c + 1, u1_ref)
        return carry

    lax.fori_loop(0, chunks // 2, body, 0)
    down(chunks - 1, u0_ref)
    o_ref[0] = _layer_norm(DN_ALPHA * x + acc_ref[...], g_ref[...], b_ref[...])


def _ffn(x1, wa, wg, cw, wd, g, b):
    B, S, _ = x1.shape
    rows = min(FFN_ROWS, S)
    pad = BF16_ROWS
    x_spec = pl.BlockSpec((1, rows, D_MODEL), lambda bi, i: (bi, i, 0))
    halo_spec = pl.BlockSpec((1, pad, D_MODEL), lambda bi, i: (bi, jnp.maximum(i * (rows // pad) - 1, 0), 0))
    return pl.pallas_call(
        _ffn_kernel,
        grid=(B, S // rows),
        in_specs=[x_spec, halo_spec, _const_spec(wa.shape), _const_spec(wg.shape), _const_spec(cw.shape),
                  _const_spec(wd.shape), _const_spec(g.shape), _const_spec(b.shape)],
        out_specs=x_spec,
        out_shape=jax.ShapeDtypeStruct(x1.shape, F32),
        scratch_shapes=[pltpu.VMEM((rows, D_MODEL), F32),
                        pltpu.VMEM((2, pad + rows, FFN_CHUNK), F32), pltpu.VMEM((2, pad + rows, FFN_CHUNK), F32)],
        compiler_params=_params("parallel", "parallel"),
        name="conv_ffn_ln",
    )(x1, x1, wa, wg, cw, wd, g, b)


def _pad_cols(w, width):
    return jnp.pad(w, ((0, 0), (0, width - w.shape[1])))


def _head_groups(w):
    rank, heads, e = w.shape
    return jnp.pad(w, ((0, 0), (0, 0), (0, LANES - e))).reshape(rank, heads * LANES)


def _prepare(w_in, g_cq, g_ckv, w_uq, w_uk, w_uv, w_o, ln1_g, ln1_b, w_up, conv_w, conv_b, w_down, ln2_g, ln2_b):
    r0, r1, r2 = MLA_Q_RANK, MLA_Q_RANK + MLA_KV_RANK, MLA_Q_RANK + MLA_KV_RANK + MLA_ROPE_DIM
    w_in_p = jnp.concatenate([w_in[:, :r1], _pad_cols(w_in[:, r1:r2], LANES), w_in[:, r2:]], axis=1).astype(BF16)

    half = MLA_ROPE_DIM // 2
    rope = w_uq[:, :, MLA_NOPE_DIM:]
    swapped = jnp.concatenate([-rope[:, :, half:], rope[:, :, :half]], axis=-1)
    swapped = jnp.concatenate([jnp.zeros_like(w_uq[:, :, :MLA_NOPE_DIM]), swapped], axis=-1)
    wq2 = jnp.concatenate([_head_groups(w_uq), _head_groups(swapped)], axis=1).astype(BF16)

    wk = _head_groups(w_uk).astype(BF16)
    wvt = jnp.pad(w_uv.transpose(1, 2, 0), ((0, 0), (0, LANES - HEAD_DIM), (0, 0))).astype(BF16)
    vone = jnp.zeros((LANES, 1), F32).at[HEAD_DIM, 0].set(1.0)

    n_chunks = D_FF // FFN_CHUNK
    wa = w_up[:, :D_FF].reshape(D_MODEL, n_chunks, FFN_CHUNK).transpose(1, 0, 2).astype(BF16)
    wg = w_up[:, D_FF:].reshape(D_MODEL, n_chunks, FFN_CHUNK).transpose(1, 0, 2).astype(BF16)
    taps = jnp.concatenate([conv_w, conv_b[None, :]], axis=0)
    cw = jnp.concatenate([0.5 * taps[:, :D_FF].reshape(4, n_chunks, FFN_CHUNK),
                          taps[:, D_FF:].reshape(4, n_chunks, FFN_CHUNK)], axis=0).transpose(1, 0, 2)
    wd = w_down.reshape(n_chunks, FFN_CHUNK, D_MODEL).astype(BF16)
    src = jnp.arange(LANES)[:, None]
    dst = jnp.arange(len(DIL_PAIRS) * DIL_WIDTH)[None, :]
    expand = (src == dst // HEAD_DIM).astype(BF16)
    row = lambda a: a.reshape(1, -1)
    return dict(w_in_p=w_in_p, gq=row(g_cq), gkv=row(g_ckv), wq2=wq2, wk=wk, wvt=wvt, vone=vone,
                w_o=w_o.astype(BF16), expand=expand, ln1=(row(ln1_g), row(ln1_b)), wa=wa, wg=wg, cw=cw, wd=wd,
                ln2=(row(ln2_g), row(ln2_b)))


def _freq_lanes():
    half = MLA_ROPE_DIM // 2
    freqs = ROPE_THETA ** (-jnp.arange(half, dtype=F32) / half)
    zeros = jnp.zeros((MLA_NOPE_DIM,), F32)
    return jnp.concatenate([zeros, freqs, freqs, jnp.zeros((LANES - MLA_NOPE_DIM - MLA_ROPE_DIM,), F32)])[None, :]


def kernel(x, w_in, g_cq, g_ckv, w_uq, w_uk, w_uv, w_o, ln1_g, ln1_b, w_up, conv_w, conv_b, w_down, ln2_g, ln2_b):
    B, S, _ = x.shape
    p = _prepare(w_in, g_cq, g_ckv, w_uq, w_uk, w_uv, w_o, ln1_g, ln1_b, w_up, conv_w, conv_b, w_down, ln2_g, ln2_b)
    scale = math.log2(math.e) / math.sqrt(MLA_NOPE_DIM + MLA_ROPE_DIM)
    tables = _rope_tables(S, _freq_lanes(), scale)
    q, k, vt, qkv_dil = _projection(x, p["w_in_p"], p["gq"], p["gkv"], p["wq2"], p["wk"], p["wvt"], tables,
                                    p["vone"])
    o_mla = _mla_attention(q, k, vt)
    o_dils, lse_all = _dilated_attention(qkv_dil)
    x1 = _out_projection(o_mla, o_dils, lse_all, x, p["w_o"], p["expand"], *p["ln1"])
    return _ffn(x1, p["wa"], p["wg"], p["cw"], p["wd"], *p["ln2"])
```

```python
import functools
import math

import jax
import jax.numpy as jnp
from jax import lax
from jax.experimental import pallas as pl
from jax.experimental.pallas import tpu as pltpu

D_MODEL = 1024
HEAD_DIM = 64
MLA_HEADS = 8
MLA_Q_RANK = 256
MLA_KV_RANK = 128
MLA_NOPE_DIM = 64
MLA_ROPE_DIM = 32
ROPE_THETA = 10000.0
DIL_HEADS = 8
DIL_PAIRS = ((128, 1), (512, 4), (2048, 16))
DIL_BLOCK = 128
DIL_WIDTH = DIL_HEADS * HEAD_DIM
D_FF = 2816
CONV_WIDTH = 3
DEPTH = 1
DN_ALPHA = (2.0 * DEPTH) ** 0.25
LN_EPS = 1e-5
RMS_EPS = 1e-6

LANES = 128
BF16_ROWS = 16
VMEM_LIMIT = 56 * 1024 * 1024

PROJ_ROWS = 512
DIL_SPAN = 2048
FFN_ROWS = 512
FFN_CHUNK = 256
NEG_BIG = 1e30

BF16 = jnp.bfloat16
F32 = jnp.float32


def _dot(a, b):
    return jnp.dot(a, b, preferred_element_type=F32)


def _dot_nt(a, b):
    return lax.dot_general(a, b, (((1,), (1,)), ((), ())), preferred_element_type=F32)


def _params(*sem, flags=None):
    return pltpu.CompilerParams(dimension_semantics=sem, vmem_limit_bytes=VMEM_LIMIT, flags=flags)


def _const_spec(shape):
    zeros = (0,) * len(shape)
    return pl.BlockSpec(shape, lambda *_: zeros, pipeline_mode=pl.Buffered(1))


def _rope_table_kernel(freq_ref, out_ref, *, scale):
    rows = out_ref.shape[1]
    pos = lax.broadcasted_iota(jnp.int32, (rows, LANES), 0).astype(F32)
    ang = pos * freq_ref[...]
    c = jnp.cos(ang)
    s = jnp.sin(ang)
    out_ref[0] = c
    out_ref[1] = s
    out_ref[2] = c * scale
    out_ref[3] = s * scale


def _rope_tables(seq, freq_lanes, scale):
    return pl.pallas_call(
        functools.partial(_rope_table_kernel, scale=scale),
        out_shape=jax.ShapeDtypeStruct((4, seq, LANES), F32),
        name="rope_tables",
    )(freq_lanes)


def _rms(x, g):
    ms = jnp.mean(x * x, axis=-1, keepdims=True)
    return x * lax.rsqrt(ms + RMS_EPS) * g


def _proj_kernel(x_ref, w_in_ref, gq_ref, gkv_ref, wq_ref, wk_ref, wvt_ref, tab_ref, vone_ref,
                 q_ref, k_ref, vt_ref, *rest):
    dil_refs, stage_refs = rest[:-2], rest[-2:]
    rows = x_ref.shape[1]
    xb = x_ref[0].astype(BF16)
    h = _dot(xb, w_in_ref[...])
    cq = _rms(h[:, :MLA_Q_RANK], gq_ref[...]).astype(BF16)
    ckv = _rms(h[:, MLA_Q_RANK:MLA_Q_RANK + MLA_KV_RANK], gkv_ref[...]).astype(BF16)
    kr = h[:, 3 * LANES:4 * LANES]
    cos, sin, cos_q, sin_q = tab_ref[0], tab_ref[1], tab_ref[2], tab_ref[3]

    k_plain = pltpu.roll(kr, 64, 1)
    k_swap = pltpu.roll(kr, 80, 1) - pltpu.roll(kr, 48, 1)
    k_rope = k_plain * cos + k_swap * sin

    q2 = _dot(cq, wq_ref[...])
    kn = _dot(ckv, wk_ref[...])
    hw = MLA_HEADS * LANES
    for hd in range(MLA_HEADS):
        lo = hd * LANES
        qh = q2[:, lo:lo + LANES] * cos_q + q2[:, hw + lo:hw + lo + LANES] * sin_q
        q_ref[0, hd] = qh.astype(BF16)
        k_ref[0, hd] = (kn[:, lo:lo + LANES] + k_rope).astype(BF16)
        vt_ref[0, hd, 0] = (_dot_nt(wvt_ref[hd], ckv) + vone_ref[...]).astype(BF16)

    base = 4 * LANES
    groups = DIL_WIDTH // LANES
    assert [d for _, d in DIL_PAIRS] == [1, 4, 16]
    flat_ref, by4_ref = stage_refs
    for c in range(3 * groups):
        slab = h[:, base + c * LANES:base + (c + 1) * LANES]
        slab = slab * (math.log2(math.e) / math.sqrt(HEAD_DIM)) if c < groups else slab
        lanes = slice((c % groups) * LANES, (c % groups + 1) * LANES)
        flat_ref[c] = slab
        dil_refs[c // groups][0, 0, :, lanes] = slab.astype(BF16)
        for r4 in range(4):
            piece = flat_ref[c, pl.ds(r4, rows // 4, stride=4), :]
            by4_ref[c, r4] = piece
            dil_refs[3 + c // groups][0, r4, :, lanes] = piece.astype(BF16)
            for r in range(4):
                piece16 = by4_ref[c, r4, pl.ds(r, rows // 16, stride=4), :]
                dil_refs[6 + c // groups][0, r4 + 4 * r, :, lanes] = piece16.astype(BF16)


def _projection(x, w_in_p, gq, gkv, wq2, wk, wvt, tables, vone):
    B, S, _ = x.shape
    rows = min(PROJ_ROWS, S)
    head_shape = jax.ShapeDtypeStruct((B, MLA_HEADS, S, LANES), BF16)
    vt_shape = jax.ShapeDtypeStruct((B, MLA_HEADS, S // rows, LANES, rows), BF16)
    head_spec = pl.BlockSpec((1, MLA_HEADS, rows, LANES), lambda b, i: (b, 0, i, 0))
    vt_spec = pl.BlockSpec((1, MLA_HEADS, 1, LANES, rows), lambda b, i: (b, 0, i, 0, 0))
    dil_shapes, dil_specs = [], []
    for _, dil in DIL_PAIRS:
        assert rows % (dil * BF16_ROWS) == 0
        dil_shapes += [jax.ShapeDtypeStruct((B, dil, S // dil, DIL_WIDTH), BF16)] * 3
        dil_specs += [pl.BlockSpec((1, dil, rows // dil, DIL_WIDTH), lambda b, i: (b, 0, i, 0))] * 3
    slabs = 3 * DIL_WIDTH // LANES
    outs = pl.pallas_call(
        _proj_kernel,
        grid=(B, S // rows),
        in_specs=[
            pl.BlockSpec((1, rows, D_MODEL), lambda b, i: (b, i, 0)),
            _const_spec(w_in_p.shape),
            _const_spec(gq.shape),
            _const_spec(gkv.shape),
            _const_spec(wq2.shape),
            _const_spec(wk.shape),
            _const_spec(wvt.shape),
            pl.BlockSpec((4, rows, LANES), lambda b, i: (0, i, 0)),
            _const_spec(vone.shape),
        ],
        out_specs=[head_spec, head_spec, vt_spec] + dil_specs,
        out_shape=[head_shape, head_shape, vt_shape] + dil_shapes,
        scratch_shapes=[pltpu.VMEM((slabs, rows, LANES), F32),
                        pltpu.VMEM((slabs, 4, rows // 4, LANES), F32)],
        compiler_params=_params("parallel", "parallel"),
        name="in_projection",
    )(x, w_in_p, gq, gkv, wq2, wk, wvt, tables, vone)
    return outs[0], outs[1], outs[2], [outs[3 + 3 * d:6 + 3 * d] for d in range(len(DIL_PAIRS))]


def _mla_kernel(q_ref, k_ref, vt_ref, o_ref, acc_ref, m_ref, st_ref, pt_ref, *, blk):
    nq = q_ref.shape[2] // blk
    strip = 64
    unroll = 4
    m_ref[...] = jnp.full_like(m_ref, -NEG_BIG)
    acc_ref[...] = jnp.zeros_like(acc_ref)

    def rows(i):
        return pl.ds(i * blk if isinstance(i, int) else pl.multiple_of(i * blk, blk), blk)

    def scores(item, slot):
        qi, j = item
        for hh in range(2):
            st_ref[hh, slot] = _dot_nt(k_ref[0, hh, rows(j), :], q_ref[0, hh, rows(qi), :])

    def update(item, slot, masked):
        qi, j = item

        def band(hh, s):
            tile = st_ref[hh, slot, s * strip:(s + 1) * strip, :]
            if masked:
                key = lax.broadcasted_iota(jnp.int32, (strip, blk), 0) + s * strip
                qry = lax.broadcasted_iota(jnp.int32, (strip, blk), 1)
                tile = jnp.where(key <= qry, tile, -NEG_BIG)
            return tile

        for hh in range(2):
            top = None
            for s in range(blk // strip):
                part = band(hh, s).reshape(strip // 8, 8, blk).max(axis=0)
                top = part if top is None else jnp.maximum(top, part)
            m_old = m_ref[qi, hh]
            m_new = jnp.maximum(m_old, jnp.max(top, axis=0, keepdims=True))
            for s in range(blk // strip):
                pt_ref[hh, s * strip:(s + 1) * strip, :] = jnp.exp2(band(hh, s) - m_new).astype(BF16)
            acc_ref[qi, hh] = acc_ref[qi, hh] * jnp.exp2(m_old - m_new) + _dot(vt_ref[0, hh, j], pt_ref[hh])
            m_ref[qi, hh] = m_new

    def sweep(items, advance, masked):
        n = len(items)
        peel = (n - 1) % unroll
        scores(items[0], 0)
        for t in range(peel):
            scores(items[t + 1], (t + 1) % 2)
            update(items[t], t % 2, masked)

        def body(_, cur):
            for k in range(unroll):
                nxt = advance(cur)
                scores(nxt, (peel + k + 1) % 2)
                update(cur, (peel + k) % 2, masked)
                cur = nxt
            return cur

        start = (jnp.int32(items[peel][0]), jnp.int32(items[peel][1]))
        lax.fori_loop(0, (n - 1) // unroll, body, start)
        update(items[n - 1], (n - 1) % 2, masked)

    def next_below(item):
        qi, j = item
        wrap = j + 1 >= qi
        return jnp.where(wrap, qi + 1, qi), jnp.where(wrap, 0, j + 1)

    sweep([(qi, j) for qi in range(1, nq) for j in range(qi)], next_below, False)
    sweep([(qi, qi) for qi in range(nq)], lambda item: (item[0] + 1, item[1] + 1), True)
    for qi in range(nq):
        outs = []
        for hh in range(2):
            acc = acc_ref[qi, hh]
            outs.append(acc[:HEAD_DIM] / acc[HEAD_DIM:HEAD_DIM + 1])
        o_ref[0, rows(qi), :] = jnp.concatenate(outs, axis=0).T.astype(o_ref.dtype)


def _mla_attention(q, k, vt):
    B, H, S, _ = q.shape
    blk = vt.shape[-1]
    qk_spec = pl.BlockSpec((1, 2, S, LANES), lambda b, hp: (b, hp, 0, 0))
    vt_spec = pl.BlockSpec((1, 2, S // blk, LANES, blk), lambda b, hp: (b, hp, 0, 0, 0))
    return pl.pallas_call(
        functools.partial(_mla_kernel, blk=blk),
        grid=(B, H // 2),
        in_specs=[qk_spec, qk_spec, vt_spec],
        out_specs=pl.BlockSpec((1, S, LANES), lambda b, hp: (b, 0, hp)),
        out_shape=jax.ShapeDtypeStruct((B, S, H * HEAD_DIM), BF16),
        scratch_shapes=[pltpu.VMEM((S // blk, 2, LANES, blk), F32),
                        pltpu.VMEM((S // blk, 2, 1, blk), F32),
                        pltpu.VMEM((2, 2, blk, blk), F32),
                        pltpu.VMEM((2, blk, blk), BF16)],
        compiler_params=_params("parallel", "parallel"),
        name="mla_attention",
    )(q, k, vt)


def _dil_bias_tiles(bias_ref, dil):
    blk = DIL_BLOCK
    key = lax.broadcasted_iota(jnp.int32, (2 * blk, blk), 0)
    qry = lax.broadcasted_iota(jnp.int32, (2 * blk, blk), 1)
    off = qry + blk - key
    steps = jnp.where((off >= 0) & (off <= blk), off.astype(F32), NEG_BIG)
    for pair in range(DIL_HEADS // 2):
        halves = []
        for hh in range(2):
            slope = 2.0 ** (-8.0 * (2 * pair + hh + 1) / DIL_HEADS)
            halves.append(steps * (-slope * dil * math.log2(math.e)))
        bias_ref[pair] = jnp.concatenate(halves, axis=1)


def _dil_needs_copy(dil, nb):
    return not (nb == 1 or dil == 1)


def _dil_kernel(q_ref, kc_ref, kp_ref, vc_ref, vp_ref, o_ref, lse_ref, st_scr, o_scr, bias_ref, *bufs, dil):
    blk = DIL_BLOCK
    pairs = DIL_HEADS // 2
    nb = q_ref.shape[2] // blk
    units = dil * nb
    _dil_bias_tiles(bias_ref, dil)
    if bufs:
        for buf, prev_ref, cur_ref in zip(bufs, (kp_ref, vp_ref), (kc_ref, vc_ref)):
            buf[:, :blk, :] = prev_ref[0]
            buf[:, blk:, :] = cur_ref[0]
    first_span = pl.program_id(1) == 0
    lane = lax.broadcasted_iota(jnp.int32, (blk, LANES), 1)
    first = lane < HEAD_DIM

    def split(u):
        if nb == 1:
            return u, 0
        if isinstance(u, int):
            return u // nb, u % nb
        return lax.shift_right_logical(u, nb.bit_length() - 1), u & (nb - 1)

    def row0(i):
        return i * blk if isinstance(i, int) else pl.multiple_of(i * blk, blk)

    def window(which, r, i, cols):
        cur_ref, prev_ref = ((kc_ref, kp_ref), (vc_ref, vp_ref))[which]
        if bufs:
            return bufs[which][r, pl.ds(row0(i), 2 * blk), cols]
        if isinstance(i, int) and i == 0:
            return jnp.concatenate([prev_ref[0, r, :, cols], cur_ref[0, r, 0:blk, cols]], axis=0)
        return cur_ref[0, r, pl.ds(row0(i - 1), 2 * blk), cols]

    def scores(u, slot):
        r, i = split(u)
        q = q_ref[0, r, pl.ds(row0(i), blk), :]
        for pair in range(pairs):
            cols = slice(pair * LANES, (pair + 1) * LANES)
            qp = q[:, cols]
            zero = jnp.zeros_like(qp)
            q2 = jnp.concatenate([jnp.where(first, qp, zero), jnp.where(first, zero, qp)], axis=0)
            st_scr[slot, pair] = _dot_nt(window(0, r, i, cols), q2) + bias_ref[pair]

    def finish(u, slot):
        r, i = split(u)
        pen = jnp.where(jnp.logical_and(first_span, i == 0), -NEG_BIG, 0.0)
        if dil == 1:
            rows = pl.ds(row0(i), blk)
        else:
            rows = pl.ds(r + i * (blk * dil), blk, stride=dil)
        for pair in range(pairs):
            cols = slice(pair * LANES, (pair + 1) * LANES)
            st = jnp.concatenate([st_scr[slot, pair, :blk, :] + pen, st_scr[slot, pair, blk:, :]], axis=0)
            m = jnp.max(st, axis=0, keepdims=True)
            pt = jnp.exp2(st - m)
            l = jnp.sum(pt, axis=0, keepdims=True)
            res = lax.dot_general(window(1, r, i, cols), pt.astype(BF16), (((0,), (0,)), ((), ())),
                                  preferred_element_type=F32)
            inv = 1.0 / l
            ot = jnp.concatenate([res[:HEAD_DIM, :LANES] * inv[:, :LANES],
                                  res[HEAD_DIM:, LANES:] * inv[:, LANES:]], axis=0)
            o_scr[pair, rows, :] = ot.T
            lse2 = m + jnp.log2(l)
            lse_ref[0, r, i, 2 * pair:2 * pair + 1, :] = lse2[:, :LANES]
            lse_ref[0, r, i, 2 * pair + 1:2 * pair + 2, :] = lse2[:, LANES:]

    assert units % 2 == 0
    scores(0, 0)
    scores(1, 1)
    finish(0, 0)

    def body(j, carry):
        u = 2 * j + 1
        scores(u + 1, 0)
        finish(u, 1)
        scores(u + 2, 1)
        finish(u + 1, 0)
        return carry

    lax.fori_loop(0, (units - 2) // 2, body, 0)
    finish(units - 1, 1)
    for pair in range(pairs):
        o_ref[0, :, pair * LANES:(pair + 1) * LANES] = o_scr[pair].astype(o_ref.dtype)


def _dil_branch(q, k, v, dil):
    B, _, M, W = q.shape
    S = M * dil
    span = min(DIL_SPAN, S)
    assert span % (dil * DIL_BLOCK) == 0 and S % span == 0
    rows = span // dil
    nb = rows // DIL_BLOCK
    cur = pl.BlockSpec((1, dil, rows, W), lambda b, c: (b, 0, c, 0))
    prev = pl.BlockSpec((1, dil, DIL_BLOCK, W), lambda b, c: (b, 0, jnp.maximum(c * nb - 1, 0), 0))
    o, lse = pl.pallas_call(
        functools.partial(_dil_kernel, dil=dil),
        grid=(B, S // span),
        in_specs=[cur, cur, prev, cur, prev],
        out_specs=[pl.BlockSpec((1, span, W), lambda b, c: (b, c, 0)),
                   pl.BlockSpec((1, dil, nb, DIL_HEADS, LANES), lambda b, c: (b, 0, c, 0, 0))],
        out_shape=[jax.ShapeDtypeStruct((B, S, W), BF16),
                   jax.ShapeDtypeStruct((B, dil, M // DIL_BLOCK, DIL_HEADS, LANES), F32)],
        scratch_shapes=[pltpu.VMEM((2, DIL_HEADS // 2, 2 * DIL_BLOCK, 2 * LANES), F32),
                        pltpu.VMEM((DIL_HEADS // 2, span, LANES), F32),
                        pltpu.VMEM((DIL_HEADS // 2, 2 * DIL_BLOCK, 2 * LANES), F32)]
        + [pltpu.VMEM((dil, DIL_BLOCK + rows, W), BF16)] * (2 if _dil_needs_copy(dil, nb) else 0),
        compiler_params=_params("parallel", "arbitrary"),
        name=f"dilated_d{dil}",
    )(q, k, k, v, v)
    return o, lse.transpose(0, 2, 4, 1, 3).reshape(B, S, DIL_HEADS)


def _dilated_attention(qkv_per_dil):
    outs, lses = [], []
    for (window, dil), (q, k, v) in zip(DIL_PAIRS, qkv_per_dil):
        assert window // dil == DIL_BLOCK
        o, lse = _dil_branch(q, k, v, dil)
        outs.append(o)
        lses.append(lse)
    lse_all = jnp.concatenate(lses, axis=-1)
    return outs, jnp.pad(lse_all, ((0, 0), (0, 0), (0, LANES - lse_all.shape[-1])))


def _layer_norm(y, g, b):
    mu = jnp.mean(y, axis=-1, keepdims=True)
    d = y - mu
    var = jnp.mean(d * d, axis=-1, keepdims=True)
    return d * lax.rsqrt(var + LN_EPS) * g + b


def _out_proj_kernel(om_ref, o0_ref, o1_ref, o2_ref, lse_ref, x_ref, wo_ref, ex_ref, g_ref, b_ref, o_ref):
    H, W = DIL_HEADS, DIL_WIDTH
    l0 = lse_ref[0]
    l1 = pltpu.roll(l0, LANES - H, 1)
    l2 = pltpu.roll(l0, LANES - 2 * H, 1)
    top = jnp.maximum(l0, jnp.maximum(l1, l2))
    e0, e1, e2 = jnp.exp2(l0 - top), jnp.exp2(l1 - top), jnp.exp2(l2 - top)
    inv = 1.0 / (e0 + e1 + e2)
    lane = lax.broadcasted_iota(jnp.int32, l0.shape, 1)
    w = jnp.where(lane < H, e0 * inv,
                  jnp.where(lane < 2 * H, pltpu.roll(e1 * inv, H, 1), pltpu.roll(e2 * inv, 2 * H, 1)))
    wide = _dot(w.astype(BF16), ex_ref[...])
    o_dil = (wide[:, :W] * o0_ref[0] + wide[:, W:2 * W] * o1_ref[0] + wide[:, 2 * W:] * o2_ref[0]).astype(BF16)
    half = om_ref.shape[-1]
    mix = _dot(om_ref[0], wo_ref[:half, :]) + _dot(o_dil, wo_ref[half:, :])
    o_ref[0] = _layer_norm(DN_ALPHA * x_ref[0] + mix, g_ref[...], b_ref[...])


def _out_projection(o_mla, o_dils, lse_all, x, w_o, expand, g, b):
    B, S, _ = x.shape
    rows = min(PROJ_ROWS, S)
    half_spec = pl.BlockSpec((1, rows, o_mla.shape[-1]), lambda bi, i: (bi, i, 0))
    lse_spec = pl.BlockSpec((1, rows, LANES), lambda bi, i: (bi, i, 0))
    x_spec = pl.BlockSpec((1, rows, D_MODEL), lambda bi, i: (bi, i, 0))
    return pl.pallas_call(
        _out_proj_kernel,
        grid=(B, S // rows),
        in_specs=[half_spec, half_spec, half_spec, half_spec, lse_spec, x_spec, _const_spec(w_o.shape),
                  _const_spec(expand.shape), _const_spec(g.shape), _const_spec(b.shape)],
        out_specs=x_spec,
        out_shape=jax.ShapeDtypeStruct(x.shape, F32),
        compiler_params=_params("parallel", "parallel"),
        name="out_projection_ln",
    )(o_mla, *o_dils, lse_all, x, w_o, expand, g, b)


def _ffn_kernel(x_ref, halo_ref, wup_ref, cw_ref, wd_ref, g_ref, b_ref, o_ref, acc_ref, u0_ref, u1_ref):
    rows = x_ref.shape[1]
    pad = halo_ref.shape[1]
    chunks = D_FF // FFN_CHUNK

    def cols(c, part=0):
        start = c * FFN_CHUNK + part * D_FF
        return pl.ds(start if isinstance(c, int) else pl.multiple_of(start, FFN_CHUNK), FFN_CHUNK)

    x = x_ref[0]
    halo = jnp.where(pl.program_id(1) == 0, 0.0, halo_ref[0])
    xe = jnp.concatenate([halo, x], axis=0).astype(BF16)
    acc_ref[...] = jnp.zeros_like(acc_ref)

    def up(c, u_ref):
        u_ref[0] = _dot(xe, wup_ref[:, cols(c, 0)])
        u_ref[1] = _dot(xe, wup_ref[:, cols(c, 1)])

    def conv(u_ref, part, taps):
        y = taps[3:4, :]
        for j in range(CONV_WIDTH):
            shift = CONV_WIDTH - 1 - j
            y = y + taps[j:j + 1, :] * u_ref[part, pl.ds(pad - shift, rows), :]
        return y

    def down(c, u_ref):
        ya = conv(u_ref, 0, cw_ref[:, cols(c, 0)])
        yg = conv(u_ref, 1, cw_ref[:, cols(c, 1)])
        c1 = math.sqrt(2.0 / math.pi)
        th = jnp.tanh(yg * (c1 + (c1 * 0.044715) * (yg * yg)))
        hidden = ((yg + yg * th) * ya).astype(BF16)
        acc_ref[...] += _dot(hidden, wd_ref[cols(c), :])

    assert chunks % 2 == 1
    up(0, u0_ref)

    def body(j, carry):
        c = 2 * j
        up(c + 1, u1_ref)
        down(c, u0_ref)
        up(c + 2, u0_ref)
        down(c + 1, u1_ref)
        return carry

    lax.fori_loop(0, chunks // 2, body, 0)
    down(chunks - 1, u0_ref)
    o_ref[0] = _layer_norm(DN_ALPHA * x + acc_ref[...], g_ref[...], b_ref[...])


def _ffn(x1, w_up, cw, wd, g, b):
    B, S, _ = x1.shape
    rows = min(FFN_ROWS, S)
    pad = BF16_ROWS
    x_spec = pl.BlockSpec((1, rows, D_MODEL), lambda bi, i: (bi, i, 0))
    halo_spec = pl.BlockSpec((1, pad, D_MODEL), lambda bi, i: (bi, jnp.maximum(i * (rows // pad) - 1, 0), 0))
    return pl.pallas_call(
        _ffn_kernel,
        grid=(B, S // rows),
        in_specs=[x_spec, halo_spec, _const_spec(w_up.shape), _const_spec(cw.shape),
                  _const_spec(wd.shape), _const_spec(g.shape), _const_spec(b.shape)],
        out_specs=x_spec,
        out_shape=jax.ShapeDtypeStruct(x1.shape, F32),
        scratch_shapes=[pltpu.VMEM((rows, D_MODEL), F32),
                        pltpu.VMEM((2, pad + rows, FFN_CHUNK), F32), pltpu.VMEM((2, pad + rows, FFN_CHUNK), F32)],
        compiler_params=_params("parallel", "parallel"),
        name="conv_ffn_ln",
    )(x1, x1, w_up, cw, wd, g, b)


def _pad_cols(w, width):
    return jnp.pad(w, ((0, 0), (0, width - w.shape[1])))


def _head_groups(w):
    rank, heads, e = w.shape
    return jnp.pad(w, ((0, 0), (0, 0), (0, LANES - e))).reshape(rank, heads * LANES)


def _prepare(w_in, g_cq, g_ckv, w_uq, w_uk, w_uv, w_o, ln1_g, ln1_b, w_up, conv_w, conv_b, w_down, ln2_g, ln2_b):
    r0, r1, r2 = MLA_Q_RANK, MLA_Q_RANK + MLA_KV_RANK, MLA_Q_RANK + MLA_KV_RANK + MLA_ROPE_DIM
    w_in_p = jnp.concatenate([w_in[:, :r1], _pad_cols(w_in[:, r1:r2], LANES), w_in[:, r2:]], axis=1).astype(BF16)

    half = MLA_ROPE_DIM // 2
    rope = w_uq[:, :, MLA_NOPE_DIM:]
    swapped = jnp.concatenate([-rope[:, :, half:], rope[:, :, :half]], axis=-1)
    swapped = jnp.concatenate([jnp.zeros_like(w_uq[:, :, :MLA_NOPE_DIM]), swapped], axis=-1)
    wq2 = jnp.concatenate([_head_groups(w_uq), _head_groups(swapped)], axis=1).astype(BF16)

    wk = _head_groups(w_uk).astype(BF16)
    wvt = jnp.pad(w_uv.transpose(1, 2, 0), ((0, 0), (0, LANES - HEAD_DIM), (0, 0))).astype(BF16)
    vone = jnp.zeros((LANES, 1), F32).at[HEAD_DIM, 0].set(1.0)

    assert D_FF % FFN_CHUNK == 0
    taps = jnp.concatenate([conv_w, conv_b[None, :]], axis=0)
    cw = jnp.concatenate([0.5 * taps[:, :D_FF], taps[:, D_FF:]], axis=1)
    src = jnp.arange(LANES)[:, None]
    dst = jnp.arange(len(DIL_PAIRS) * DIL_WIDTH)[None, :]
    expand = (src == dst // HEAD_DIM).astype(BF16)
    row = lambda a: a.reshape(1, -1)
    return dict(w_in_p=w_in_p, gq=row(g_cq), gkv=row(g_ckv), wq2=wq2, wk=wk, wvt=wvt, vone=vone,
                w_o=w_o.astype(BF16), expand=expand, ln1=(row(ln1_g), row(ln1_b)), w_up=w_up.astype(BF16), cw=cw,
                wd=w_down.astype(BF16),
                ln2=(row(ln2_g), row(ln2_b)))


def _freq_lanes():
    half = MLA_ROPE_DIM // 2
    freqs = ROPE_THETA ** (-jnp.arange(half, dtype=F32) / half)
    zeros = jnp.zeros((MLA_NOPE_DIM,), F32)
    return jnp.concatenate([zeros, freqs, freqs, jnp.zeros((LANES - MLA_NOPE_DIM - MLA_ROPE_DIM,), F32)])[None, :]


def kernel(x, w_in, g_cq, g_ckv, w_uq, w_uk, w_uv, w_o, ln1_g, ln1_b, w_up, conv_w, conv_b, w_down, ln2_g, ln2_b):
    B, S, _ = x.shape
    p = _prepare(w_in, g_cq, g_ckv, w_uq, w_uk, w_uv, w_o, ln1_g, ln1_b, w_up, conv_w, conv_b, w_down, ln2_g, ln2_b)
    scale = math.log2(math.e) / math.sqrt(MLA_NOPE_DIM + MLA_ROPE_DIM)
    tables = _rope_tables(S, _freq_lanes(), scale)
    q, k, vt, qkv_dil = _projection(x, p["w_in_p"], p["gq"], p["gkv"], p["wq2"], p["wk"], p["wvt"], tables,
                                    p["vone"])
    o_mla = _mla_attention(q, k, vt)
    o_dils, lse_all = _dilated_attention(qkv_dil)
    x1 = _out_projection(o_mla, o_dils, lse_all, x, p["w_o"], p["expand"], *p["ln1"])
    return _ffn(x1, p["w_up"], p["cw"], p["wd"], *p["ln2"])
```

```python
import functools
import math

import jax
import jax.numpy as jnp
from jax import lax
from jax.experimental import pallas as pl
from jax.experimental.pallas import tpu as pltpu

D_MODEL = 1024
HEAD_DIM = 64
MLA_HEADS = 8
MLA_Q_RANK = 256
MLA_KV_RANK = 128
MLA_NOPE_DIM = 64
MLA_ROPE_DIM = 32
ROPE_THETA = 10000.0
DIL_HEADS = 8
DIL_PAIRS = ((128, 1), (512, 4), (2048, 16))
DIL_BLOCK = 128
DIL_WIDTH = DIL_HEADS * HEAD_DIM
D_FF = 2816
CONV_WIDTH = 3
DEPTH = 1
DN_ALPHA = (2.0 * DEPTH) ** 0.25
LN_EPS = 1e-5
RMS_EPS = 1e-6

LANES = 128
BF16_ROWS = 16
VMEM_LIMIT = 56 * 1024 * 1024

PROJ_ROWS = 512
DIL_SPAN = 2048
FFN_ROWS = 512
FFN_CHUNK = 256
NEG_BIG = 1e30

BF16 = jnp.bfloat16
F32 = jnp.float32


def _dot(a, b):
    return jnp.dot(a, b, preferred_element_type=F32)


def _dot_nt(a, b):
    return lax.dot_general(a, b, (((1,), (1,)), ((), ())), preferred_element_type=F32)


def _params(*sem, flags=None):
    return pltpu.CompilerParams(dimension_semantics=sem, vmem_limit_bytes=VMEM_LIMIT, flags=flags)


def _const_spec(shape):
    zeros = (0,) * len(shape)
    return pl.BlockSpec(shape, lambda *_: zeros, pipeline_mode=pl.Buffered(1))


def _rope_table_kernel(freq_ref, out_ref, *, scale):
    rows = out_ref.shape[1]
    pos = lax.broadcasted_iota(jnp.int32, (rows, LANES), 0).astype(F32)
    ang = pos * freq_ref[...]
    c = jnp.cos(ang)
    s = jnp.sin(ang)
    out_ref[0] = c
    out_ref[1] = s
    out_ref[2] = c * scale
    out_ref[3] = s * scale


def _rope_tables(seq, freq_lanes, scale):
    return pl.pallas_call(
        functools.partial(_rope_table_kernel, scale=scale),
        out_shape=jax.ShapeDtypeStruct((4, seq, LANES), F32),
        name="rope_tables",
    )(freq_lanes)


def _rms(x, g):
    ms = jnp.mean(x * x, axis=-1, keepdims=True)
    return x * lax.rsqrt(ms + RMS_EPS) * g


def _proj_kernel(x_ref, w_in_ref, gq_ref, gkv_ref, wq_ref, wk_ref, wvt_ref, tab_ref, vone_ref,
                 q_ref, k_ref, vt_ref, *rest):
    dil_refs, stage_refs = rest[:-2], rest[-2:]
    rows = x_ref.shape[1]
    xb = x_ref[0].astype(BF16)
    h = _dot(xb, w_in_ref[...])
    cq = _rms(h[:, :MLA_Q_RANK], gq_ref[...]).astype(BF16)
    ckv = _rms(h[:, MLA_Q_RANK:MLA_Q_RANK + MLA_KV_RANK], gkv_ref[...]).astype(BF16)
    kr = h[:, 3 * LANES:4 * LANES]
    cos, sin, cos_q, sin_q = tab_ref[0], tab_ref[1], tab_ref[2], tab_ref[3]

    k_plain = pltpu.roll(kr, 64, 1)
    k_swap = pltpu.roll(kr, 80, 1) - pltpu.roll(kr, 48, 1)
    k_rope = k_plain * cos + k_swap * sin

    q2 = _dot(cq, wq_ref[...])
    kn = _dot(ckv, wk_ref[...])
    hw = MLA_HEADS * LANES
    for hd in range(MLA_HEADS):
        lo = hd * LANES
        qh = q2[:, lo:lo + LANES] * cos_q + q2[:, hw + lo:hw + lo + LANES] * sin_q
        q_ref[0, hd] = qh.astype(BF16)
        k_ref[0, hd] = (kn[:, lo:lo + LANES] + k_rope).astype(BF16)
        vt_ref[0, hd, 0] = (_dot_nt(wvt_ref[hd], ckv) + vone_ref[...]).astype(BF16)

    base = 4 * LANES
    groups = DIL_WIDTH // LANES
    assert [d for _, d in DIL_PAIRS] == [1, 4, 16]
    flat_ref, by4_ref = stage_refs
    for c in range(3 * groups):
        slab = h[:, base + c * LANES:base + (c + 1) * LANES]
        slab = slab * (math.log2(math.e) / math.sqrt(HEAD_DIM)) if c < groups else slab
        lanes = slice((c % groups) * LANES, (c % groups + 1) * LANES)
        flat_ref[c] = slab
        dil_refs[c // groups][0, 0, :, lanes] = slab.astype(BF16)
        for r4 in range(4):
            piece = flat_ref[c, pl.ds(r4, rows // 4, stride=4), :]
            by4_ref[c, r4] = piece
            dil_refs[3 + c // groups][0, r4, :, lanes] = piece.astype(BF16)
            for r in range(4):
                piece16 = by4_ref[c, r4, pl.ds(r, rows // 16, stride=4), :]
                dil_refs[6 + c // groups][0, r4 + 4 * r, :, lanes] = piece16.astype(BF16)


def _projection(x, w_in_p, gq, gkv, wq2, wk, wvt, tables, vone):
    B, S, _ = x.shape
    rows = min(PROJ_ROWS, S)
    head_shape = jax.ShapeDtypeStruct((B, MLA_HEADS, S, LANES), BF16)
    vt_shape = jax.ShapeDtypeStruct((B, MLA_HEADS, S // rows, LANES, rows), BF16)
    head_spec = pl.BlockSpec((1, MLA_HEADS, rows, LANES), lambda b, i: (b, 0, i, 0))
    vt_spec = pl.BlockSpec((1, MLA_HEADS, 1, LANES, rows), lambda b, i: (b, 0, i, 0, 0))
    dil_shapes, dil_specs = [], []
    for _, dil in DIL_PAIRS:
        assert rows % (dil * BF16_ROWS) == 0
        dil_shapes += [jax.ShapeDtypeStruct((B, dil, S // dil, DIL_WIDTH), BF16)] * 3
        dil_specs += [pl.BlockSpec((1, dil, rows // dil, DIL_WIDTH), lambda b, i: (b, 0, i, 0))] * 3
    slabs = 3 * DIL_WIDTH // LANES
    outs = pl.pallas_call(
        _proj_kernel,
        grid=(B, S // rows),
        in_specs=[
            pl.BlockSpec((1, rows, D_MODEL), lambda b, i: (b, i, 0)),
            _const_spec(w_in_p.shape),
            _const_spec(gq.shape),
            _const_spec(gkv.shape),
            _const_spec(wq2.shape),
            _const_spec(wk.shape),
            _const_spec(wvt.shape),
            pl.BlockSpec((4, rows, LANES), lambda b, i: (0, i, 0)),
            _const_spec(vone.shape),
        ],
        out_specs=[head_spec, head_spec, vt_spec] + dil_specs,
        out_shape=[head_shape, head_shape, vt_shape] + dil_shapes,
        scratch_shapes=[pltpu.VMEM((slabs, rows, LANES), F32),
                        pltpu.VMEM((slabs, 4, rows // 4, LANES), F32)],
        compiler_params=_params("parallel", "parallel"),
        name="in_projection",
    )(x, w_in_p, gq, gkv, wq2, wk, wvt, tables, vone)
    return outs[0], outs[1], outs[2], [outs[3 + 3 * d:6 + 3 * d] for d in range(len(DIL_PAIRS))]


def _mla_kernel(q_ref, k_ref, vt_ref, o_ref, acc_ref, m_ref, st_ref, pt_ref, *, blk):
    nq = q_ref.shape[2] // blk
    strip = 64
    unroll = 4
    m_ref[...] = jnp.full_like(m_ref, -NEG_BIG)
    acc_ref[...] = jnp.zeros_like(acc_ref)

    def rows(i):
        return pl.ds(i * blk if isinstance(i, int) else pl.multiple_of(i * blk, blk), blk)

    def scores(item, slot):
        qi, j = item
        for hh in range(2):
            st_ref[hh, slot] = _dot_nt(k_ref[0, hh, rows(j), :], q_ref[0, hh, rows(qi), :])

    def update(item, slot, masked):
        qi, j = item

        def band(hh, s):
            tile = st_ref[hh, slot, s * strip:(s + 1) * strip, :]
            if masked:
                key = lax.broadcasted_iota(jnp.int32, (strip, blk), 0) + s * strip
                qry = lax.broadcasted_iota(jnp.int32, (strip, blk), 1)
                tile = jnp.where(key <= qry, tile, -NEG_BIG)
            return tile

        for hh in range(2):
            top = None
            for s in range(blk // strip):
                part = band(hh, s).reshape(strip // 8, 8, blk).max(axis=0)
                top = part if top is None else jnp.maximum(top, part)
            m_old = m_ref[qi, hh]
            m_new = jnp.maximum(m_old, jnp.max(top, axis=0, keepdims=True))
            for s in range(blk // strip):
                pt_ref[hh, s * strip:(s + 1) * strip, :] = jnp.exp2(band(hh, s) - m_new).astype(BF16)
            acc_ref[qi, hh] = acc_ref[qi, hh] * jnp.exp2(m_old - m_new) + _dot(vt_ref[0, hh, j], pt_ref[hh])
            m_ref[qi, hh] = m_new

    def sweep(items, advance, masked):
        n = len(items)
        peel = (n - 1) % unroll
        scores(items[0], 0)
        for t in range(peel):
            scores(items[t + 1], (t + 1) % 2)
            update(items[t], t % 2, masked)

        def body(_, cur):
            for k in range(unroll):
                nxt = advance(cur)
                scores(nxt, (peel + k + 1) % 2)
                update(cur, (peel + k) % 2, masked)
                cur = nxt
            return cur

        start = (jnp.int32(items[peel][0]), jnp.int32(items[peel][1]))
        lax.fori_loop(0, (n - 1) // unroll, body, start)
        update(items[n - 1], (n - 1) % 2, masked)

    def next_below(item):
        qi, j = item
        wrap = j + 1 >= qi
        return jnp.where(wrap, qi + 1, qi), jnp.where(wrap, 0, j + 1)

    sweep([(qi, j) for qi in range(1, nq) for j in range(qi)], next_below, False)
    sweep([(qi, qi) for qi in range(nq)], lambda item: (item[0] + 1, item[1] + 1), True)
    for qi in range(nq):
        outs = []
        for hh in range(2):
            acc = acc_ref[qi, hh]
            outs.append(acc[:HEAD_DIM] / acc[HEAD_DIM:HEAD_DIM + 1])
        o_ref[0, rows(qi), :] = jnp.concatenate(outs, axis=0).T.astype(o_ref.dtype)


def _mla_attention(q, k, vt):
    B, H, S, _ = q.shape
    blk = vt.shape[-1]
    qk_spec = pl.BlockSpec((1, 2, S, LANES), lambda b, hp: (b, hp, 0, 0))
    vt_spec = pl.BlockSpec((1, 2, S // blk, LANES, blk), lambda b, hp: (b, hp, 0, 0, 0))
    return pl.pallas_call(
        functools.partial(_mla_kernel, blk=blk),
        grid=(B, H // 2),
        in_specs=[qk_spec, qk_spec, vt_spec],
        out_specs=pl.BlockSpec((1, S, LANES), lambda b, hp: (b, 0, hp)),
        out_shape=jax.ShapeDtypeStruct((B, S, H * HEAD_DIM), BF16),
        scratch_shapes=[pltpu.VMEM((S // blk, 2, LANES, blk), F32),
                        pltpu.VMEM((S // blk, 2, 1, blk), F32),
                        pltpu.VMEM((2, 2, blk, blk), F32),
                        pltpu.VMEM((2, blk, blk), BF16)],
        compiler_params=_params("parallel", "parallel"),
        name="mla_attention",
    )(q, k, vt)


def _dil_bias_tiles(bias_ref, dil):
    blk = DIL_BLOCK
    key = lax.broadcasted_iota(jnp.int32, (2 * blk, blk), 0)
    qry = lax.broadcasted_iota(jnp.int32, (2 * blk, blk), 1)
    off = qry + blk - key
    steps = jnp.where((off >= 0) & (off <= blk), off.astype(F32), NEG_BIG)
    for pair in range(DIL_HEADS // 2):
        halves = []
        for hh in range(2):
            slope = 2.0 ** (-8.0 * (2 * pair + hh + 1) / DIL_HEADS)
            halves.append(steps * (-slope * dil * math.log2(math.e)))
        tile = jnp.concatenate(halves, axis=1)
        bias_ref[0, pair] = tile
        bias_ref[1, pair, :blk, :] = jnp.full((blk, 2 * blk), -NEG_BIG, F32)
        bias_ref[1, pair, blk:, :] = tile[blk:]


def _dil_needs_copy(dil, nb):
    return not (nb == 1 or dil == 1)


def _dil_kernel(q_ref, kc_ref, kp_ref, vc_ref, vp_ref, o_ref, lse_ref, st_scr, o_scr, lse_scr, bias_ref, *bufs,
                dil, lane0):
    blk = DIL_BLOCK
    pairs = DIL_HEADS // 2
    nb = q_ref.shape[2] // blk
    units = dil * nb
    _dil_bias_tiles(bias_ref, dil)
    if bufs:
        for buf, prev_ref, cur_ref in zip(bufs, (kp_ref, vp_ref), (kc_ref, vc_ref)):
            buf[:, :blk, :] = prev_ref[0]
            buf[:, blk:, :] = cur_ref[0]
    first_span = pl.program_id(1) == 0
    lane = lax.broadcasted_iota(jnp.int32, (blk, LANES), 1)
    first = lane < HEAD_DIM

    def split(u):
        if nb == 1:
            return u, 0
        if isinstance(u, int):
            return u // nb, u % nb
        return lax.shift_right_logical(u, nb.bit_length() - 1), u & (nb - 1)

    def row0(i):
        return i * blk if isinstance(i, int) else pl.multiple_of(i * blk, blk)

    def window(which, r, i, cols):
        cur_ref, prev_ref = ((kc_ref, kp_ref), (vc_ref, vp_ref))[which]
        if bufs:
            return bufs[which][r, pl.ds(row0(i), 2 * blk), cols]
        if isinstance(i, int) and i == 0:
            return jnp.concatenate([prev_ref[0, r, :, cols], cur_ref[0, r, 0:blk, cols]], axis=0)
        return cur_ref[0, r, pl.ds(row0(i - 1), 2 * blk), cols]

    def scores(u, slot):
        r, i = split(u)
        q = q_ref[0, r, pl.ds(row0(i), blk), :]
        no_prev = jnp.logical_and(first_span, i == 0).astype(jnp.int32)
        for pair in range(pairs):
            cols = slice(pair * LANES, (pair + 1) * LANES)
            qp = q[:, cols]
            zero = jnp.zeros_like(qp)
            q2 = jnp.concatenate([jnp.where(first, qp, zero), jnp.where(first, zero, qp)], axis=0)
            st_scr[slot, pair] = _dot_nt(window(0, r, i, cols), q2) + bias_ref[no_prev, pair]

    def finish(u, slot):
        r, i = split(u)
        if dil == 1:
            rows = pl.ds(row0(i), blk)
        else:
            rows = pl.ds(r + i * (blk * dil), blk, stride=dil)
        lse_rows = []
        for pair in range(pairs):
            cols = slice(pair * LANES, (pair + 1) * LANES)
            st = st_scr[slot, pair]
            m = jnp.max(st, axis=0, keepdims=True)
            pt = jnp.exp2(st - m).astype(BF16)
            vt = jnp.concatenate([window(1, r, i, cols).T, jnp.ones((BF16_ROWS, 2 * blk), BF16)], axis=0)
            res = _dot(vt, pt)
            l = res[LANES:LANES + 1, :]
            inv = 1.0 / l
            ot = jnp.concatenate([res[:HEAD_DIM, :LANES] * inv[:, :LANES],
                                  res[HEAD_DIM:LANES, LANES:] * inv[:, LANES:]], axis=0)
            o_scr[pair, rows, :] = ot.T
            lse2 = m + jnp.log2(l)
            lse_rows += [lse2[:, :LANES], lse2[:, LANES:]]
        tile = jnp.concatenate([jnp.zeros((lane0, blk), F32)] * (lane0 > 0) + lse_rows
                               + [jnp.zeros((LANES - lane0 - DIL_HEADS, blk), F32)], axis=0)
        lse_scr[rows, :] = tile.T

    assert units % 2 == 0
    scores(0, 0)
    scores(1, 1)
    finish(0, 0)

    def body(j, carry):
        u = 2 * j + 1
        scores(u + 1, 0)
        finish(u, 1)
        scores(u + 2, 1)
        finish(u + 1, 0)
        return carry

    lax.fori_loop(0, (units - 2) // 2, body, 0)
    finish(units - 1, 1)
    for pair in range(pairs):
        o_ref[0, :, pair * LANES:(pair + 1) * LANES] = o_scr[pair].astype(o_ref.dtype)
    lse_ref[0] = lse_scr[...]


def _dil_branch(q, k, v, dil, lane0):
    B, _, M, W = q.shape
    S = M * dil
    span = min(DIL_SPAN, S)
    assert span % (dil * DIL_BLOCK) == 0 and S % span == 0
    rows = span // dil
    nb = rows // DIL_BLOCK
    cur = pl.BlockSpec((1, dil, rows, W), lambda b, c: (b, 0, c, 0))
    prev = pl.BlockSpec((1, dil, DIL_BLOCK, W), lambda b, c: (b, 0, jnp.maximum(c * nb - 1, 0), 0))
    return pl.pallas_call(
        functools.partial(_dil_kernel, dil=dil, lane0=lane0),
        grid=(B, S // span),
        in_specs=[cur, cur, prev, cur, prev],
        out_specs=[pl.BlockSpec((1, span, W), lambda b, c: (b, c, 0)),
                   pl.BlockSpec((1, span, LANES), lambda b, c: (b, c, 0))],
        out_shape=[jax.ShapeDtypeStruct((B, S, W), BF16), jax.ShapeDtypeStruct((B, S, LANES), F32)],
        scratch_shapes=[pltpu.VMEM((2, DIL_HEADS // 2, 2 * DIL_BLOCK, 2 * LANES), F32),
                        pltpu.VMEM((DIL_HEADS // 2, span, LANES), F32),
                        pltpu.VMEM((span, LANES), F32),
                        pltpu.VMEM((2, DIL_HEADS // 2, 2 * DIL_BLOCK, 2 * LANES), F32)]
        + [pltpu.VMEM((dil, DIL_BLOCK + rows, W), BF16)] * (2 if _dil_needs_copy(dil, nb) else 0),
        compiler_params=_params("parallel", "arbitrary"),
        name=f"dilated_d{dil}",
    )(q, k, k, v, v)


def _dilated_attention(qkv_per_dil):
    outs, lses = [], []
    for i, ((window, dil), (q, k, v)) in enumerate(zip(DIL_PAIRS, qkv_per_dil)):
        assert window // dil == DIL_BLOCK
        o, lse = _dil_branch(q, k, v, dil, i * DIL_HEADS)
        outs.append(o)
        lses.append(lse)
    return outs, lses


def _layer_norm(y, g, b):
    mu = jnp.mean(y, axis=-1, keepdims=True)
    d = y - mu
    var = jnp.mean(d * d, axis=-1, keepdims=True)
    return d * lax.rsqrt(var + LN_EPS) * g + b


def _out_proj_kernel(om_ref, o0_ref, o1_ref, o2_ref, lse0_ref, lse1_ref, lse2_ref, x_ref, wo_ref, ex_ref,
                     g_ref, b_ref, o_ref):
    H, W = DIL_HEADS, DIL_WIDTH
    l0 = lse0_ref[0] + lse1_ref[0] + lse2_ref[0]
    l1 = pltpu.roll(l0, LANES - H, 1)
    l2 = pltpu.roll(l0, LANES - 2 * H, 1)
    top = jnp.maximum(l0, jnp.maximum(l1, l2))
    e0, e1, e2 = jnp.exp2(l0 - top), jnp.exp2(l1 - top), jnp.exp2(l2 - top)
    inv = 1.0 / (e0 + e1 + e2)
    lane = lax.broadcasted_iota(jnp.int32, l0.shape, 1)
    w = jnp.where(lane < H, e0 * inv,
                  jnp.where(lane < 2 * H, pltpu.roll(e1 * inv, H, 1), pltpu.roll(e2 * inv, 2 * H, 1)))
    wide = _dot(w.astype(BF16), ex_ref[...])
    o_dil = (wide[:, :W] * o0_ref[0] + wide[:, W:2 * W] * o1_ref[0] + wide[:, 2 * W:] * o2_ref[0]).astype(BF16)
    half = om_ref.shape[-1]
    mix = _dot(om_ref[0], wo_ref[:half, :]) + _dot(o_dil, wo_ref[half:, :])
    o_ref[0] = _layer_norm(DN_ALPHA * x_ref[0] + mix, g_ref[...], b_ref[...])


def _out_projection(o_mla, o_dils, lses, x, w_o, expand, g, b):
    B, S, _ = x.shape
    rows = min(PROJ_ROWS, S)
    half_spec = pl.BlockSpec((1, rows, o_mla.shape[-1]), lambda bi, i: (bi, i, 0))
    lse_spec = pl.BlockSpec((1, rows, LANES), lambda bi, i: (bi, i, 0))
    x_spec = pl.BlockSpec((1, rows, D_MODEL), lambda bi, i: (bi, i, 0))
    return pl.pallas_call(
        _out_proj_kernel,
        grid=(B, S // rows),
        in_specs=[half_spec, half_spec, half_spec, half_spec, lse_spec, lse_spec, lse_spec, x_spec,
                  _const_spec(w_o.shape), _const_spec(expand.shape), _const_spec(g.shape), _const_spec(b.shape)],
        out_specs=x_spec,
        out_shape=jax.ShapeDtypeStruct(x.shape, F32),
        compiler_params=_params("parallel", "parallel"),
        name="out_projection_ln",
    )(o_mla, *o_dils, *lses, x, w_o, expand, g, b)


def _ffn_kernel(x_ref, halo_ref, wup_ref, cw_ref, wd_ref, g_ref, b_ref, o_ref, acc_ref, u0_ref, u1_ref, xe_ref):
    rows = x_ref.shape[1]
    pad = halo_ref.shape[1]
    chunks = D_FF // FFN_CHUNK

    def cols(c, part=0):
        start = c * FFN_CHUNK + part * D_FF
        return pl.ds(start if isinstance(c, int) else pl.multiple_of(start, FFN_CHUNK), FFN_CHUNK)

    xe_ref[:pad, :] = jnp.where(pl.program_id(1) == 0, 0.0, halo_ref[0]).astype(BF16)
    xe_ref[pad:, :] = x_ref[0].astype(BF16)
    acc_ref[...] = jnp.zeros_like(acc_ref)

    def up(c, u_ref):
        u_ref[0] = _dot(xe_ref[...], wup_ref[:, cols(c, 0)])
        u_ref[1] = _dot(xe_ref[...], wup_ref[:, cols(c, 1)])

    def conv(u_ref, part, taps):
        y = taps[3:4, :]
        for j in range(CONV_WIDTH):
            shift = CONV_WIDTH - 1 - j
            y = y + taps[j:j + 1, :] * u_ref[part, pl.ds(pad - shift, rows), :]
        return y

    def down(c, u_ref):
        ya = conv(u_ref, 0, cw_ref[:, cols(c, 0)])
        yg = conv(u_ref, 1, cw_ref[:, cols(c, 1)])
        c1 = math.sqrt(2.0 / math.pi)
        th = jnp.tanh(yg * (c1 + (c1 * 0.044715) * (yg * yg)))
        hidden = ((yg + yg * th) * ya).astype(BF16)
        acc_ref[...] += _dot(hidden, wd_ref[cols(c), :])

    assert chunks % 2 == 1
    up(0, u0_ref)

    def body(j, carry):
        c = 2 * j
        up(c + 1, u1_ref)
        down(c, u0_ref)
        up(c + 2, u0_ref)
        down(c + 1, u1_ref)
        return carry

    lax.fori_loop(0, chunks // 2, body, 0)
    down(chunks - 1, u0_ref)
    o_ref[0] = _layer_norm(DN_ALPHA * x_ref[0] + acc_ref[...], g_ref[...], b_ref[...])


def _ffn(x1, w_up, cw, wd, g, b):
    B, S, _ = x1.shape
    rows = min(FFN_ROWS, S)
    pad = BF16_ROWS
    x_spec = pl.BlockSpec((1, rows, D_MODEL), lambda bi, i: (bi, i, 0))
    halo_spec = pl.BlockSpec((1, pad, D_MODEL), lambda bi, i: (bi, jnp.maximum(i * (rows // pad) - 1, 0), 0))
    return pl.pallas_call(
        _ffn_kernel,
        grid=(B, S // rows),
        in_specs=[x_spec, halo_spec, _const_spec(w_up.shape), _const_spec(cw.shape),
                  _const_spec(wd.shape), _const_spec(g.shape), _const_spec(b.shape)],
        out_specs=x_spec,
        out_shape=jax.ShapeDtypeStruct(x1.shape, F32),
        scratch_shapes=[pltpu.VMEM((rows, D_MODEL), F32),
                        pltpu.VMEM((2, pad + rows, FFN_CHUNK), F32), pltpu.VMEM((2, pad + rows, FFN_CHUNK), F32),
                        pltpu.VMEM((pad + rows, D_MODEL), BF16)],
        compiler_params=_params("parallel", "parallel"),
        name="conv_ffn_ln",
    )(x1, x1, w_up, cw, wd, g, b)


def _pad_cols(w, width):
    return jnp.pad(w, ((0, 0), (0, width - w.shape[1])))


def _head_groups(w):
    rank, heads, e = w.shape
    return jnp.pad(w, ((0, 0), (0, 0), (0, LANES - e))).reshape(rank, heads * LANES)


def _prepare(w_in, g_cq, g_ckv, w_uq, w_uk, w_uv, w_o, ln1_g, ln1_b, w_up, conv_w, conv_b, w_down, ln2_g, ln2_b):
    r0, r1, r2 = MLA_Q_RANK, MLA_Q_RANK + MLA_KV_RANK, MLA_Q_RANK + MLA_KV_RANK + MLA_ROPE_DIM
    w_in_p = jnp.concatenate([w_in[:, :r1], _pad_cols(w_in[:, r1:r2], LANES), w_in[:, r2:]], axis=1).astype(BF16)

    half = MLA_ROPE_DIM // 2
    rope = w_uq[:, :, MLA_NOPE_DIM:]
    swapped = jnp.concatenate([-rope[:, :, half:], rope[:, :, :half]], axis=-1)
    swapped = jnp.concatenate([jnp.zeros_like(w_uq[:, :, :MLA_NOPE_DIM]), swapped], axis=-1)
    wq2 = jnp.concatenate([_head_groups(w_uq), _head_groups(swapped)], axis=1).astype(BF16)

    wk = _head_groups(w_uk).astype(BF16)
    wvt = jnp.pad(w_uv.transpose(1, 2, 0), ((0, 0), (0, LANES - HEAD_DIM), (0, 0))).astype(BF16)
    vone = jnp.zeros((LANES, 1), F32).at[HEAD_DIM, 0].set(1.0)

    assert D_FF % FFN_CHUNK == 0
    taps = jnp.concatenate([conv_w, conv_b[None, :]], axis=0)
    cw = jnp.concatenate([0.5 * taps[:, :D_FF], taps[:, D_FF:]], axis=1)
    src = jnp.arange(LANES)[:, None]
    dst = jnp.arange(len(DIL_PAIRS) * DIL_WIDTH)[None, :]
    expand = (src == dst // HEAD_DIM).astype(BF16)
    row = lambda a: a.reshape(1, -1)
    return dict(w_in_p=w_in_p, gq=row(g_cq), gkv=row(g_ckv), wq2=wq2, wk=wk, wvt=wvt, vone=vone,
                w_o=w_o.astype(BF16), expand=expand, ln1=(row(ln1_g), row(ln1_b)), w_up=w_up.astype(BF16), cw=cw,
                wd=w_down.astype(BF16),
                ln2=(row(ln2_g), row(ln2_b)))


def _freq_lanes():
    half = MLA_ROPE_DIM // 2
    freqs = ROPE_THETA ** (-jnp.arange(half, dtype=F32) / half)
    zeros = jnp.zeros((MLA_NOPE_DIM,), F32)
    return jnp.concatenate([zeros, freqs, freqs, jnp.zeros((LANES - MLA_NOPE_DIM - MLA_ROPE_DIM,), F32)])[None, :]


def kernel(x, w_in, g_cq, g_ckv, w_uq, w_uk, w_uv, w_o, ln1_g, ln1_b, w_up, conv_w, conv_b, w_down, ln2_g, ln2_b):
    B, S, _ = x.shape
    p = _prepare(w_in, g_cq, g_ckv, w_uq, w_uk, w_uv, w_o, ln1_g, ln1_b, w_up, conv_w, conv_b, w_down, ln2_g, ln2_b)
    scale = math.log2(math.e) / math.sqrt(MLA_NOPE_DIM + MLA_ROPE_DIM)
    tables = _rope_tables(S, _freq_lanes(), scale)
    q, k, vt, qkv_dil = _projection(x, p["w_in_p"], p["gq"], p["gkv"], p["wq2"], p["wk"], p["wvt"], tables,
                                    p["vone"])
    o_mla = _mla_attention(q, k, vt)
    o_dils, lses = _dilated_attention(qkv_dil)
    x1 = _out_projection(o_mla, o_dils, lses, x, p["w_o"], p["expand"], *p["ln1"])
    return _ffn(x1, p["w_up"], p["cw"], p["wd"], *p["ln2"])
```

```python
import functools
import math

import jax
import jax.numpy as jnp
from jax import lax
from jax.experimental import pallas as pl
from jax.experimental.pallas import tpu as pltpu

D_MODEL = 1024
HEAD_DIM = 64
MLA_HEADS = 8
MLA_Q_RANK = 256
MLA_KV_RANK = 128
MLA_NOPE_DIM = 64
MLA_ROPE_DIM = 32
ROPE_THETA = 10000.0
DIL_HEADS = 8
DIL_PAIRS = ((128, 1), (512, 4), (2048, 16))
DIL_BLOCK = 128
DIL_WIDTH = DIL_HEADS * HEAD_DIM
D_FF = 2816
CONV_WIDTH = 3
DEPTH = 1
DN_ALPHA = (2.0 * DEPTH) ** 0.25
LN_EPS = 1e-5
RMS_EPS = 1e-6

LANES = 128
BF16_ROWS = 16
F32_ROWS = 8
VMEM_LIMIT = 56 * 1024 * 1024

PROJ_ROWS = 512
DIL_SPAN = 2048
FFN_ROWS = 512
FFN_CHUNK = 256
NEG_BIG = 1e30

BF16 = jnp.bfloat16
F32 = jnp.float32


def _dot(a, b):
    return jnp.dot(a, b, preferred_element_type=F32)


def _dot_nt(a, b):
    return lax.dot_general(a, b, (((1,), (1,)), ((), ())), preferred_element_type=F32)


def _params(*sem, flags=None):
    return pltpu.CompilerParams(dimension_semantics=sem, vmem_limit_bytes=VMEM_LIMIT, flags=flags)


def _const_spec(shape):
    zeros = (0,) * len(shape)
    return pl.BlockSpec(shape, lambda *_: zeros, pipeline_mode=pl.Buffered(1))


def _rope_table_kernel(freq_ref, out_ref, *, scale):
    rows = out_ref.shape[1]
    pos = lax.broadcasted_iota(jnp.int32, (rows, LANES), 0).astype(F32)
    ang = pos * freq_ref[...]
    c = jnp.cos(ang)
    s = jnp.sin(ang)
    out_ref[0] = c
    out_ref[1] = s
    out_ref[2] = c * scale
    out_ref[3] = s * scale


def _rope_tables(seq, freq_lanes, scale):
    return pl.pallas_call(
        functools.partial(_rope_table_kernel, scale=scale),
        out_shape=jax.ShapeDtypeStruct((4, seq, LANES), F32),
        name="rope_tables",
    )(freq_lanes)


def _rms(x, g):
    ms = jnp.mean(x * x, axis=-1, keepdims=True)
    return x * lax.rsqrt(ms + RMS_EPS) * g


def _proj_kernel(x_ref, w_in_ref, gq_ref, gkv_ref, wq_ref, wk_ref, wvt_ref, tab_ref, vone_ref,
                 q_ref, k_ref, vt_ref, *rest):
    dil_refs, stage_refs = rest[:-2], rest[-2:]
    rows = x_ref.shape[1]
    xb = x_ref[0].astype(BF16)
    h = _dot(xb, w_in_ref[...])
    cq = _rms(h[:, :MLA_Q_RANK], gq_ref[...]).astype(BF16)
    ckv = _rms(h[:, MLA_Q_RANK:MLA_Q_RANK + MLA_KV_RANK], gkv_ref[...]).astype(BF16)
    kr = h[:, 3 * LANES:4 * LANES]
    cos, sin, cos_q, sin_q = tab_ref[0], tab_ref[1], tab_ref[2], tab_ref[3]

    k_plain = pltpu.roll(kr, 64, 1)
    k_swap = pltpu.roll(kr, 80, 1) - pltpu.roll(kr, 48, 1)
    k_rope = k_plain * cos + k_swap * sin

    q2 = _dot(cq, wq_ref[...])
    kn = _dot(ckv, wk_ref[...])
    hw = MLA_HEADS * LANES
    for hd in range(MLA_HEADS):
        lo = hd * LANES
        qh = q2[:, lo:lo + LANES] * cos_q + q2[:, hw + lo:hw + lo + LANES] * sin_q
        q_ref[0, hd] = qh.astype(BF16)
        k_ref[0, hd] = (kn[:, lo:lo + LANES] + k_rope).astype(BF16)
        vt_ref[0, hd, 0] = (_dot_nt(wvt_ref[hd], ckv) + vone_ref[...]).astype(BF16)

    base = 4 * LANES
    groups = DIL_WIDTH // LANES
    assert [d for _, d in DIL_PAIRS] == [1, 4, 16]
    flat_ref, by4_ref = stage_refs
    for c in range(3 * groups):
        slab = h[:, base + c * LANES:base + (c + 1) * LANES]
        slab = slab * (math.log2(math.e) / math.sqrt(HEAD_DIM)) if c < groups else slab
        lanes = slice((c % groups) * LANES, (c % groups + 1) * LANES)
        flat_ref[c] = slab
        dil_refs[c // groups][0, 0, :, lanes] = slab.astype(BF16)
        for r4 in range(4):
            piece = flat_ref[c, pl.ds(r4, rows // 4, stride=4), :]
            by4_ref[c, r4] = piece
            dil_refs[3 + c // groups][0, r4, :, lanes] = piece.astype(BF16)
            for r in range(4):
                piece16 = by4_ref[c, r4, pl.ds(r, rows // 16, stride=4), :]
                dil_refs[6 + c // groups][0, r4 + 4 * r, :, lanes] = piece16.astype(BF16)


def _projection(x, w_in_p, gq, gkv, wq2, wk, wvt, tables, vone):
    B, S, _ = x.shape
    rows = min(PROJ_ROWS, S)
    head_shape = jax.ShapeDtypeStruct((B, MLA_HEADS, S, LANES), BF16)
    vt_shape = jax.ShapeDtypeStruct((B, MLA_HEADS, S // rows, LANES, rows), BF16)
    head_spec = pl.BlockSpec((1, MLA_HEADS, rows, LANES), lambda b, i: (b, 0, i, 0))
    vt_spec = pl.BlockSpec((1, MLA_HEADS, 1, LANES, rows), lambda b, i: (b, 0, i, 0, 0))
    dil_shapes, dil_specs = [], []
    for _, dil in DIL_PAIRS:
        assert rows % (dil * BF16_ROWS) == 0
        dil_shapes += [jax.ShapeDtypeStruct((B, dil, S // dil, DIL_WIDTH), BF16)] * 3
        dil_specs += [pl.BlockSpec((1, dil, rows // dil, DIL_WIDTH), lambda b, i: (b, 0, i, 0))] * 3
    slabs = 3 * DIL_WIDTH // LANES
    outs = pl.pallas_call(
        _proj_kernel,
        grid=(B, S // rows),
        in_specs=[
            pl.BlockSpec((1, rows, D_MODEL), lambda b, i: (b, i, 0)),
            _const_spec(w_in_p.shape),
            _const_spec(gq.shape),
            _const_spec(gkv.shape),
            _const_spec(wq2.shape),
            _const_spec(wk.shape),
            _const_spec(wvt.shape),
            pl.BlockSpec((4, rows, LANES), lambda b, i: (0, i, 0)),
            _const_spec(vone.shape),
        ],
        out_specs=[head_spec, head_spec, vt_spec] + dil_specs,
        out_shape=[head_shape, head_shape, vt_shape] + dil_shapes,
        scratch_shapes=[pltpu.VMEM((slabs, rows, LANES), F32),
                        pltpu.VMEM((slabs, 4, rows // 4, LANES), F32)],
        compiler_params=_params("parallel", "parallel"),
        name="in_projection",
    )(x, w_in_p, gq, gkv, wq2, wk, wvt, tables, vone)
    return outs[0], outs[1], outs[2], [outs[3 + 3 * d:6 + 3 * d] for d in range(len(DIL_PAIRS))]


def _mla_kernel(q_ref, k_ref, vt_ref, o_ref, acc_ref, m_ref, st_ref, pt_ref, *, blk):
    nq = q_ref.shape[2] // blk
    strip = 64
    unroll = 4
    m_ref[...] = jnp.full_like(m_ref, -NEG_BIG)
    acc_ref[...] = jnp.zeros_like(acc_ref)

    def rows(i):
        return pl.ds(i * blk if isinstance(i, int) else pl.multiple_of(i * blk, blk), blk)

    half = blk // 2
    assert half % strip == 0

    def half_rows(i, which):
        start = i * blk + which * half
        return pl.ds(start if isinstance(i, int) else pl.multiple_of(start, half), half)

    def scores(item, slot, masked):
        qi, j = item
        for hh in range(2):
            if not masked:
                st_ref[hh, slot] = _dot_nt(k_ref[0, hh, rows(j), :], q_ref[0, hh, rows(qi), :])
            else:
                st_ref[hh, slot, :half, :] = _dot_nt(k_ref[0, hh, half_rows(j, 0), :], q_ref[0, hh, rows(qi), :])
                st_ref[hh, slot, half:, half:] = _dot_nt(k_ref[0, hh, half_rows(j, 1), :],
                                                         q_ref[0, hh, half_rows(qi, 1), :])

    def update(item, slot, masked):
        qi, j = item

        def band(hh, s):
            if not masked:
                return st_ref[hh, slot, s * strip:(s + 1) * strip, :]
            lo = 0 if s * strip < half else half
            tile = st_ref[hh, slot, s * strip:(s + 1) * strip, lo:]
            key = lax.broadcasted_iota(jnp.int32, tile.shape, 0) + s * strip
            qry = lax.broadcasted_iota(jnp.int32, tile.shape, 1) + lo
            return jnp.where(key <= qry, tile, -NEG_BIG)

        def fold(parts):
            top = parts[0]
            for part in parts[1:]:
                top = jnp.maximum(top, part)
            return top

        for hh in range(2):
            tops = [band(hh, s).reshape(strip // 8, 8, -1).max(axis=0) for s in range(blk // strip)]
            if masked:
                n_lo = half // strip
                wide, narrow = fold(tops[:n_lo]), fold(tops[n_lo:])
                top = jnp.concatenate([wide[:, :half], jnp.maximum(wide[:, half:], narrow)], axis=1)
            else:
                top = fold(tops)
            m_old = m_ref[qi, hh]
            m_new = jnp.maximum(m_old, jnp.max(top, axis=0, keepdims=True))
            for s in range(blk // strip):
                tile = band(hh, s)
                lo = blk - tile.shape[1]
                pt_ref[hh, s * strip:(s + 1) * strip, lo:] = jnp.exp2(tile - m_new[:, lo:]).astype(BF16)
            if masked:
                vt = vt_ref[0, hh, j]
                pv = _dot(vt[:, :half], pt_ref[hh, :half, :])
                pv_hi = _dot(vt[:, half:], pt_ref[hh, half:, half:])
                pv = jnp.concatenate([pv[:, :half], pv[:, half:] + pv_hi], axis=1)
            else:
                pv = _dot(vt_ref[0, hh, j], pt_ref[hh])
            acc_ref[qi, hh] = acc_ref[qi, hh] * jnp.exp2(m_old - m_new) + pv
            m_ref[qi, hh] = m_new

    def sweep(items, advance, masked):
        n = len(items)
        peel = (n - 1) % unroll
        scores(items[0], 0, masked)
        for t in range(peel):
            scores(items[t + 1], (t + 1) % 2, masked)
            update(items[t], t % 2, masked)

        def body(_, cur):
            for k in range(unroll):
                nxt = advance(cur)
                scores(nxt, (peel + k + 1) % 2, masked)
                update(cur, (peel + k) % 2, masked)
                cur = nxt
            return cur

        start = (jnp.int32(items[peel][0]), jnp.int32(items[peel][1]))
        lax.fori_loop(0, (n - 1) // unroll, body, start)
        update(items[n - 1], (n - 1) % 2, masked)

    def next_below(item):
        qi, j = item
        wrap = j + 1 >= qi
        return jnp.where(wrap, qi + 1, qi), jnp.where(wrap, 0, j + 1)

    sweep([(qi, j) for qi in range(1, nq) for j in range(qi)], next_below, False)
    sweep([(qi, qi) for qi in range(nq)], lambda item: (item[0] + 1, item[1] + 1), True)
    for qi in range(nq):
        outs = []
        for hh in range(2):
            acc = acc_ref[qi, hh]
            outs.append(acc[:HEAD_DIM] / acc[HEAD_DIM:HEAD_DIM + 1])
        o_ref[0, rows(qi), :] = jnp.concatenate(outs, axis=0).T.astype(o_ref.dtype)


def _mla_attention(q, k, vt):
    B, H, S, _ = q.shape
    blk = vt.shape[-1]
    qk_spec = pl.BlockSpec((1, 2, S, LANES), lambda b, hp: (b, hp, 0, 0))
    vt_spec = pl.BlockSpec((1, 2, S // blk, LANES, blk), lambda b, hp: (b, hp, 0, 0, 0))
    return pl.pallas_call(
        functools.partial(_mla_kernel, blk=blk),
        grid=(B, H // 2),
        in_specs=[qk_spec, qk_spec, vt_spec],
        out_specs=pl.BlockSpec((1, S, LANES), lambda b, hp: (b, 0, hp)),
        out_shape=jax.ShapeDtypeStruct((B, S, H * HEAD_DIM), BF16),
        scratch_shapes=[pltpu.VMEM((S // blk, 2, LANES, blk), F32),
                        pltpu.VMEM((S // blk, 2, 1, blk), F32),
                        pltpu.VMEM((2, 2, blk, blk), F32),
                        pltpu.VMEM((2, blk, blk), BF16)],
        compiler_params=_params("parallel", "parallel"),
        name="mla_attention",
    )(q, k, vt)


def _dil_bias_tiles(bias_ref, dil):
    blk = DIL_BLOCK
    key = lax.broadcasted_iota(jnp.int32, (2 * blk, blk), 0)
    qry = lax.broadcasted_iota(jnp.int32, (2 * blk, blk), 1)
    off = qry + blk - key
    steps = jnp.where((off >= 0) & (off <= blk), off.astype(F32), NEG_BIG)
    for pair in range(DIL_HEADS // 2):
        halves = []
        for hh in range(2):
            slope = 2.0 ** (-8.0 * (2 * pair + hh + 1) / DIL_HEADS)
            halves.append(steps * (-slope * dil * math.log2(math.e)))
        tile = jnp.concatenate(halves, axis=1)
        bias_ref[0, pair] = tile
        bias_ref[1, pair, :blk, :] = jnp.full((blk, 2 * blk), -NEG_BIG, F32)
        bias_ref[1, pair, blk:, :] = tile[blk:]


def _dil_needs_copy(dil, nb):
    return not (nb == 1 or dil == 1)


def _dil_kernel(q_ref, kc_ref, kp_ref, vc_ref, vp_ref, o_ref, lse_ref, st_scr, o_scr, lse_scr, bias_ref, *bufs,
                dil, lane0):
    blk = DIL_BLOCK
    pairs = DIL_HEADS // 2
    nb = q_ref.shape[2] // blk
    units = dil * nb
    pl.when((pl.program_id(0) == 0) & (pl.program_id(1) == 0))(functools.partial(_dil_bias_tiles, bias_ref, dil))
    if bufs:
        for buf, prev_ref, cur_ref in zip(bufs, (kp_ref, vp_ref), (kc_ref, vc_ref)):
            buf[:, :blk, :] = prev_ref[0]
            buf[:, blk:, :] = cur_ref[0]
    first_span = pl.program_id(1) == 0
    lane = lax.broadcasted_iota(jnp.int32, (blk, LANES), 1)
    first = lane < HEAD_DIM

    def split(u):
        if nb == 1:
            return u, 0
        if isinstance(u, int):
            return u // nb, u % nb
        return lax.shift_right_logical(u, nb.bit_length() - 1), u & (nb - 1)

    def row0(i):
        return i * blk if isinstance(i, int) else pl.multiple_of(i * blk, blk)

    def window(which, r, i, cols):
        cur_ref, prev_ref = ((kc_ref, kp_ref), (vc_ref, vp_ref))[which]
        if bufs:
            return bufs[which][r, pl.ds(row0(i), 2 * blk), cols]
        if isinstance(i, int) and i == 0:
            return jnp.concatenate([prev_ref[0, r, :, cols], cur_ref[0, r, 0:blk, cols]], axis=0)
        return cur_ref[0, r, pl.ds(row0(i - 1), 2 * blk), cols]

    def scores(u, slot):
        r, i = split(u)
        q = q_ref[0, r, pl.ds(row0(i), blk), :]
        no_prev = jnp.logical_and(first_span, i == 0).astype(jnp.int32)
        for pair in range(pairs):
            cols = slice(pair * LANES, (pair + 1) * LANES)
            qp = q[:, cols]
            zero = jnp.zeros_like(qp)
            q2 = jnp.concatenate([jnp.where(first, qp, zero), jnp.where(first, zero, qp)], axis=0)
            st_scr[slot, pair] = _dot_nt(window(0, r, i, cols), q2) + bias_ref[no_prev, pair]

    def finish(u, slot):
        r, i = split(u)
        if dil == 1:
            rows = pl.ds(row0(i), blk)
        else:
            rows = pl.ds(r + i * (blk * dil), blk, stride=dil)
        lse_rows = []
        for pair in range(pairs):
            cols = slice(pair * LANES, (pair + 1) * LANES)
            st = st_scr[slot, pair]
            m = jnp.max(st, axis=0, keepdims=True)
            pt = jnp.exp2(st - m).astype(BF16)
            vt = jnp.concatenate([window(1, r, i, cols).T, jnp.ones((BF16_ROWS, 2 * blk), BF16)], axis=0)
            res = _dot(vt, pt)
            l = res[LANES:LANES + 1, :]
            inv = 1.0 / l
            ot = jnp.concatenate([res[:HEAD_DIM, :LANES] * inv[:, :LANES],
                                  res[HEAD_DIM:LANES, LANES:] * inv[:, LANES:]], axis=0)
            o_scr[pair, rows, :] = ot.T
            lse2 = m + jnp.log2(l)
            lse_rows += [lse2[:, :LANES], lse2[:, LANES:]]
        tile = jnp.concatenate([jnp.zeros((lane0, blk), F32)] * (lane0 > 0) + lse_rows
                               + [jnp.zeros((LANES - lane0 - DIL_HEADS, blk), F32)], axis=0)
        lse_scr[rows, :] = tile.T

    assert units % 2 == 0
    scores(0, 0)
    scores(1, 1)
    finish(0, 0)

    def body(j, carry):
        u = 2 * j + 1
        scores(u + 1, 0)
        finish(u, 1)
        scores(u + 2, 1)
        finish(u + 1, 0)
        return carry

    lax.fori_loop(0, (units - 2) // 2, body, 0)
    finish(units - 1, 1)
    for pair in range(pairs):
        o_ref[0, :, pair * LANES:(pair + 1) * LANES] = o_scr[pair].astype(o_ref.dtype)
    lse_ref[0] = lse_scr[...]


def _dil_branch(q, k, v, dil, lane0):
    B, _, M, W = q.shape
    S = M * dil
    span = min(DIL_SPAN, S)
    assert span % (dil * DIL_BLOCK) == 0 and S % span == 0
    rows = span // dil
    nb = rows // DIL_BLOCK
    cur = pl.BlockSpec((1, dil, rows, W), lambda b, c: (b, 0, c, 0))
    prev = pl.BlockSpec((1, dil, DIL_BLOCK, W), lambda b, c: (b, 0, jnp.maximum(c * nb - 1, 0), 0))
    return pl.pallas_call(
        functools.partial(_dil_kernel, dil=dil, lane0=lane0),
        grid=(B, S // span),
        in_specs=[cur, cur, prev, cur, prev],
        out_specs=[pl.BlockSpec((1, span, W), lambda b, c: (b, c, 0)),
                   pl.BlockSpec((1, span, LANES), lambda b, c: (b, c, 0))],
        out_shape=[jax.ShapeDtypeStruct((B, S, W), BF16), jax.ShapeDtypeStruct((B, S, LANES), F32)],
        scratch_shapes=[pltpu.VMEM((2, DIL_HEADS // 2, 2 * DIL_BLOCK, 2 * LANES), F32),
                        pltpu.VMEM((DIL_HEADS // 2, span, LANES), F32),
                        pltpu.VMEM((span, LANES), F32),
                        pltpu.VMEM((2, DIL_HEADS // 2, 2 * DIL_BLOCK, 2 * LANES), F32)]
        + [pltpu.VMEM((dil, DIL_BLOCK + rows, W), BF16)] * (2 if _dil_needs_copy(dil, nb) else 0),
        compiler_params=_params("arbitrary", "arbitrary"),
        name=f"dilated_d{dil}",
    )(q, k, k, v, v)


def _dilated_attention(qkv_per_dil):
    outs, lses = [], []
    for i, ((window, dil), (q, k, v)) in enumerate(zip(DIL_PAIRS, qkv_per_dil)):
        assert window // dil == DIL_BLOCK
        o, lse = _dil_branch(q, k, v, dil, i * DIL_HEADS)
        outs.append(o)
        lses.append(lse)
    return outs, lses


def _layer_norm(y, g, b):
    mu = jnp.mean(y, axis=-1, keepdims=True)
    d = y - mu
    var = jnp.mean(d * d, axis=-1, keepdims=True)
    return d * lax.rsqrt(var + LN_EPS) * g + b


def _out_proj_kernel(om_ref, o0_ref, o1_ref, o2_ref, lse0_ref, lse1_ref, lse2_ref, x_ref, wo_ref, ex_ref,
                     g_ref, b_ref, o_ref):
    H, W = DIL_HEADS, DIL_WIDTH
    l0 = lse0_ref[0] + lse1_ref[0] + lse2_ref[0]
    l1 = pltpu.roll(l0, LANES - H, 1)
    l2 = pltpu.roll(l0, LANES - 2 * H, 1)
    top = jnp.maximum(l0, jnp.maximum(l1, l2))
    e0, e1, e2 = jnp.exp2(l0 - top), jnp.exp2(l1 - top), jnp.exp2(l2 - top)
    inv = 1.0 / (e0 + e1 + e2)
    lane = lax.broadcasted_iota(jnp.int32, l0.shape, 1)
    w = jnp.where(lane < H, e0 * inv,
                  jnp.where(lane < 2 * H, pltpu.roll(e1 * inv, H, 1), pltpu.roll(e2 * inv, 2 * H, 1)))
    wide = _dot(w.astype(BF16), ex_ref[...])
    o_dil = (wide[:, :W] * o0_ref[0] + wide[:, W:2 * W] * o1_ref[0] + wide[:, 2 * W:] * o2_ref[0]).astype(BF16)
    half = om_ref.shape[-1]
    mix = _dot(om_ref[0], wo_ref[:half, :]) + _dot(o_dil, wo_ref[half:, :])
    o_ref[0] = _layer_norm(DN_ALPHA * x_ref[0] + mix, g_ref[...], b_ref[...])


def _out_projection(o_mla, o_dils, lses, x, w_o, expand, g, b):
    B, S, _ = x.shape
    rows = min(PROJ_ROWS, S)
    half_spec = pl.BlockSpec((1, rows, o_mla.shape[-1]), lambda bi, i: (bi, i, 0))
    lse_spec = pl.BlockSpec((1, rows, LANES), lambda bi, i: (bi, i, 0))
    x_spec = pl.BlockSpec((1, rows, D_MODEL), lambda bi, i: (bi, i, 0))
    return pl.pallas_call(
        _out_proj_kernel,
        grid=(B, S // rows),
        in_specs=[half_spec, half_spec, half_spec, half_spec, lse_spec, lse_spec, lse_spec, x_spec,
                  _const_spec(w_o.shape), _const_spec(expand.shape), _const_spec(g.shape), _const_spec(b.shape)],
        out_specs=x_spec,
        out_shape=jax.ShapeDtypeStruct(x.shape, F32),
        compiler_params=_params("parallel", "parallel"),
        name="out_projection_ln",
    )(o_mla, *o_dils, *lses, x, w_o, expand, g, b)


def _ffn_kernel(x_ref, wup_ref, cw_ref, wd_ref, g_ref, b_ref, o_ref, acc_ref, u0_ref, u1_ref, xe_ref, tail_ref):
    rows = x_ref.shape[1]
    pad = tail_ref.shape[2]
    chunks = D_FF // FFN_CHUNK

    def cols(c, part=0):
        start = c * FFN_CHUNK + part * D_FF
        return pl.ds(start if isinstance(c, int) else pl.multiple_of(start, FFN_CHUNK), FFN_CHUNK)

    @pl.when(pl.program_id(1) == 0)
    def _():
        tail_ref[...] = jnp.zeros_like(tail_ref)

    xe_ref[...] = x_ref[0].astype(BF16)
    acc_ref[...] = jnp.zeros_like(acc_ref)

    def up(c, u_ref):
        for part in range(2):
            u = _dot(xe_ref[...], wup_ref[:, cols(c, part)])
            u_ref[part, :pad, :] = tail_ref[c, part]
            u_ref[part, pad:, :] = u
            tail_ref[c, part] = u[rows - pad:, :]

    def conv(u_ref, part, taps):
        y = taps[3:4, :]
        for j in range(CONV_WIDTH):
            shift = CONV_WIDTH - 1 - j
            y = y + taps[j:j + 1, :] * u_ref[part, pl.ds(pad - shift, rows), :]
        return y

    def down(c, u_ref):
        ya = conv(u_ref, 0, cw_ref[:, cols(c, 0)])
        yg = conv(u_ref, 1, cw_ref[:, cols(c, 1)])
        c1 = math.sqrt(2.0 / math.pi)
        th = jnp.tanh(yg * (c1 + (c1 * 0.044715) * (yg * yg)))
        hidden = ((yg + yg * th) * ya).astype(BF16)
        acc_ref[...] += _dot(hidden, wd_ref[cols(c), :])

    assert chunks % 2 == 1
    up(0, u0_ref)

    def body(j, carry):
        c = 2 * j
        up(c + 1, u1_ref)
        down(c, u0_ref)
        up(c + 2, u0_ref)
        down(c + 1, u1_ref)
        return carry

    lax.fori_loop(0, chunks // 2, body, 0)
    down(chunks - 1, u0_ref)
    o_ref[0] = _layer_norm(DN_ALPHA * x_ref[0] + acc_ref[...], g_ref[...], b_ref[...])


def _ffn(x1, w_up, cw, wd, g, b):
    B, S, _ = x1.shape
    rows = min(FFN_ROWS, S)
    pad = F32_ROWS
    x_spec = pl.BlockSpec((1, rows, D_MODEL), lambda bi, i: (bi, i, 0))
    return pl.pallas_call(
        _ffn_kernel,
        grid=(B, S // rows),
        in_specs=[x_spec, _const_spec(w_up.shape), _const_spec(cw.shape),
                  _const_spec(wd.shape), _const_spec(g.shape), _const_spec(b.shape)],
        out_specs=x_spec,
        out_shape=jax.ShapeDtypeStruct(x1.shape, F32),
        scratch_shapes=[pltpu.VMEM((rows, D_MODEL), F32),
                        pltpu.VMEM((2, pad + rows, FFN_CHUNK), F32), pltpu.VMEM((2, pad + rows, FFN_CHUNK), F32),
                        pltpu.VMEM((rows, D_MODEL), BF16),
                        pltpu.VMEM((D_FF // FFN_CHUNK, 2, pad, FFN_CHUNK), F32)],
        compiler_params=_params("parallel", "arbitrary"),
        name="conv_ffn_ln",
    )(x1, w_up, cw, wd, g, b)


def _pad_cols(w, width):
    return jnp.pad(w, ((0, 0), (0, width - w.shape[1])))


def _head_groups(w):
    rank, heads, e = w.shape
    return jnp.pad(w, ((0, 0), (0, 0), (0, LANES - e))).reshape(rank, heads * LANES)


def _prepare(w_in, g_cq, g_ckv, w_uq, w_uk, w_uv, w_o, ln1_g, ln1_b, w_up, conv_w, conv_b, w_down, ln2_g, ln2_b):
    r0, r1, r2 = MLA_Q_RANK, MLA_Q_RANK + MLA_KV_RANK, MLA_Q_RANK + MLA_KV_RANK + MLA_ROPE_DIM
    w_in_p = jnp.concatenate([w_in[:, :r1], _pad_cols(w_in[:, r1:r2], LANES), w_in[:, r2:]], axis=1).astype(BF16)

    half = MLA_ROPE_DIM // 2
    rope = w_uq[:, :, MLA_NOPE_DIM:]
    swapped = jnp.concatenate([-rope[:, :, half:], rope[:, :, :half]], axis=-1)
    swapped = jnp.concatenate([jnp.zeros_like(w_uq[:, :, :MLA_NOPE_DIM]), swapped], axis=-1)
    wq2 = jnp.concatenate([_head_groups(w_uq), _head_groups(swapped)], axis=1).astype(BF16)

    wk = _head_groups(w_uk).astype(BF16)
    wvt = jnp.pad(w_uv.transpose(1, 2, 0), ((0, 0), (0, LANES - HEAD_DIM), (0, 0))).astype(BF16)
    vone = jnp.zeros((LANES, 1), F32).at[HEAD_DIM, 0].set(1.0)

    assert D_FF % FFN_CHUNK == 0
    taps = jnp.concatenate([conv_w, conv_b[None, :]], axis=0)
    cw = jnp.concatenate([0.5 * taps[:, :D_FF], taps[:, D_FF:]], axis=1)
    src = jnp.arange(LANES)[:, None]
    dst = jnp.arange(len(DIL_PAIRS) * DIL_WIDTH)[None, :]
    expand = (src == dst // HEAD_DIM).astype(BF16)
    row = lambda a: a.reshape(1, -1)
    return dict(w_in_p=w_in_p, gq=row(g_cq), gkv=row(g_ckv), wq2=wq2, wk=wk, wvt=wvt, vone=vone,
                w_o=w_o.astype(BF16), expand=expand, ln1=(row(ln1_g), row(ln1_b)), w_up=w_up.astype(BF16), cw=cw,
                wd=w_down.astype(BF16),
                ln2=(row(ln2_g), row(ln2_b)))


def _freq_lanes():
    half = MLA_ROPE_DIM // 2
    freqs = ROPE_THETA ** (-jnp.arange(half, dtype=F32) / half)
    zeros = jnp.zeros((MLA_NOPE_DIM,), F32)
    return jnp.concatenate([zeros, freqs, freqs, jnp.zeros((LANES - MLA_NOPE_DIM - MLA_ROPE_DIM,), F32)])[None, :]


def kernel(x, w_in, g_cq, g_ckv, w_uq, w_uk, w_uv, w_o, ln1_g, ln1_b, w_up, conv_w, conv_b, w_down, ln2_g, ln2_b):
    B, S, _ = x.shape
    p = _prepare(w_in, g_cq, g_ckv, w_uq, w_uk, w_uv, w_o, ln1_g, ln1_b, w_up, conv_w, conv_b, w_down, ln2_g, ln2_b)
    scale = math.log2(math.e) / math.sqrt(MLA_NOPE_DIM + MLA_ROPE_DIM)
    tables = _rope_tables(S, _freq_lanes(), scale)
    q, k, vt, qkv_dil = _projection(x, p["w_in_p"], p["gq"], p["gkv"], p["wq2"], p["wk"], p["wvt"], tables,
                                    p["vone"])
    o_mla = _mla_attention(q, k, vt)
    o_dils, lses = _dilated_attention(qkv_dil)
    x1 = _out_projection(o_mla, o_dils, lses, x, p["w_o"], p["expand"], *p["ln1"])
    return _ffn(x1, p["w_up"], p["cw"], p["wd"], *p["ln2"])
```

```python
import functools
import math

import jax
import jax.numpy as jnp
from jax import lax
from jax.experimental import pallas as pl
from jax.experimental.pallas import tpu as pltpu

D_MODEL = 1024
HEAD_DIM = 64
MLA_HEADS = 8
MLA_Q_RANK = 256
MLA_KV_RANK = 128
MLA_NOPE_DIM = 64
MLA_ROPE_DIM = 32
ROPE_THETA = 10000.0
DIL_HEADS = 8
DIL_PAIRS = ((128, 1), (512, 4), (2048, 16))
DIL_BLOCK = 128
DIL_WIDTH = DIL_HEADS * HEAD_DIM
D_FF = 2816
CONV_WIDTH = 3
DEPTH = 1
DN_ALPHA = (2.0 * DEPTH) ** 0.25
LN_EPS = 1e-5
RMS_EPS = 1e-6

LANES = 128
BF16_ROWS = 16
F32_ROWS = 8
VMEM_LIMIT = 56 * 1024 * 1024

PROJ_ROWS = 512
DIL_SPAN = 2048
FFN_ROWS = 512
FFN_CHUNK = 256
NEG_BIG = 1e30

BF16 = jnp.bfloat16
F32 = jnp.float32


def _dot(a, b):
    return jnp.dot(a, b, preferred_element_type=F32)


def _dot_nt(a, b):
    return lax.dot_general(a, b, (((1,), (1,)), ((), ())), preferred_element_type=F32)


def _params(*sem, flags=None):
    return pltpu.CompilerParams(dimension_semantics=sem, vmem_limit_bytes=VMEM_LIMIT, flags=flags)


def _const_spec(shape):
    zeros = (0,) * len(shape)
    return pl.BlockSpec(shape, lambda *_: zeros, pipeline_mode=pl.Buffered(1))


def _rope_table_kernel(freq_ref, out_ref, *, scale):
    rows = out_ref.shape[1]
    pos = lax.broadcasted_iota(jnp.int32, (rows, LANES), 0).astype(F32)
    ang = pos * freq_ref[...]
    c = jnp.cos(ang)
    s = jnp.sin(ang)
    out_ref[0] = c
    out_ref[1] = s
    out_ref[2] = c * scale
    out_ref[3] = s * scale


def _rope_tables(seq, freq_lanes, scale):
    return pl.pallas_call(
        functools.partial(_rope_table_kernel, scale=scale),
        out_shape=jax.ShapeDtypeStruct((4, seq, LANES), F32),
        name="rope_tables",
    )(freq_lanes)


def _rms(x, g):
    ms = jnp.mean(x * x, axis=-1, keepdims=True)
    return x * lax.rsqrt(ms + RMS_EPS) * g


def _proj_kernel(x_ref, w_in_ref, gq_ref, gkv_ref, wq_ref, wk_ref, wvt_ref, tab_ref, vone_ref,
                 q_ref, k_ref, vt_ref, *rest):
    dil_refs, stage_refs = rest[:-2], rest[-2:]
    rows = x_ref.shape[1]
    xb = x_ref[0].astype(BF16)
    h = _dot(xb, w_in_ref[...])
    cq = _rms(h[:, :MLA_Q_RANK], gq_ref[...]).astype(BF16)
    ckv = _rms(h[:, MLA_Q_RANK:MLA_Q_RANK + MLA_KV_RANK], gkv_ref[...]).astype(BF16)
    kr = h[:, 3 * LANES:4 * LANES]
    cos, sin, cos_q, sin_q = tab_ref[0], tab_ref[1], tab_ref[2], tab_ref[3]

    k_plain = pltpu.roll(kr, 64, 1)
    k_swap = pltpu.roll(kr, 80, 1) - pltpu.roll(kr, 48, 1)
    k_rope = k_plain * cos + k_swap * sin

    q2 = _dot(cq, wq_ref[...])
    kn = _dot(ckv, wk_ref[...])
    vt = _dot_nt(wvt_ref[...].reshape(MLA_HEADS * LANES, MLA_KV_RANK), ckv)
    hw = MLA_HEADS * LANES
    for hd in range(MLA_HEADS):
        lo = hd * LANES
        qh = q2[:, lo:lo + LANES] * cos_q + q2[:, hw + lo:hw + lo + LANES] * sin_q
        q_ref[0, hd] = qh.astype(BF16)
        k_ref[0, hd] = (kn[:, lo:lo + LANES] + k_rope).astype(BF16)
        vt_ref[0, hd, 0] = (vt[lo:lo + LANES, :] + vone_ref[...]).astype(BF16)

    base = 4 * LANES
    groups = DIL_WIDTH // LANES
    assert [d for _, d in DIL_PAIRS] == [1, 4, 16]
    flat_ref, by4_ref = stage_refs
    for c in range(3 * groups):
        slab = h[:, base + c * LANES:base + (c + 1) * LANES]
        slab = slab * (math.log2(math.e) / math.sqrt(HEAD_DIM)) if c < groups else slab
        lanes = slice((c % groups) * LANES, (c % groups + 1) * LANES)
        flat_ref[c] = slab
        dil_refs[c // groups][0, 0, :, lanes] = slab.astype(BF16)
        for r4 in range(4):
            piece = flat_ref[c, pl.ds(r4, rows // 4, stride=4), :]
            by4_ref[c, r4] = piece
            dil_refs[3 + c // groups][0, r4, :, lanes] = piece.astype(BF16)
            for r in range(4):
                piece16 = by4_ref[c, r4, pl.ds(r, rows // 16, stride=4), :]
                dil_refs[6 + c // groups][0, r4 + 4 * r, :, lanes] = piece16.astype(BF16)


def _projection(x, w_in_p, gq, gkv, wq2, wk, wvt, tables, vone):
    B, S, _ = x.shape
    rows = min(PROJ_ROWS, S)
    head_shape = jax.ShapeDtypeStruct((B, MLA_HEADS, S, LANES), BF16)
    vt_shape = jax.ShapeDtypeStruct((B, MLA_HEADS, S // rows, LANES, rows), BF16)
    head_spec = pl.BlockSpec((1, MLA_HEADS, rows, LANES), lambda b, i: (b, 0, i, 0))
    vt_spec = pl.BlockSpec((1, MLA_HEADS, 1, LANES, rows), lambda b, i: (b, 0, i, 0, 0))
    dil_shapes, dil_specs = [], []
    for _, dil in DIL_PAIRS:
        assert rows % (dil * BF16_ROWS) == 0
        dil_shapes += [jax.ShapeDtypeStruct((B, dil, S // dil, DIL_WIDTH), BF16)] * 3
        dil_specs += [pl.BlockSpec((1, dil, rows // dil, DIL_WIDTH), lambda b, i: (b, 0, i, 0))] * 3
    slabs = 3 * DIL_WIDTH // LANES
    outs = pl.pallas_call(
        _proj_kernel,
        grid=(B, S // rows),
        in_specs=[
            pl.BlockSpec((1, rows, D_MODEL), lambda b, i: (b, i, 0)),
            _const_spec(w_in_p.shape),
            _const_spec(gq.shape),
            _const_spec(gkv.shape),
            _const_spec(wq2.shape),
            _const_spec(wk.shape),
            _const_spec(wvt.shape),
            pl.BlockSpec((4, rows, LANES), lambda b, i: (0, i, 0)),
            _const_spec(vone.shape),
        ],
        out_specs=[head_spec, head_spec, vt_spec] + dil_specs,
        out_shape=[head_shape, head_shape, vt_shape] + dil_shapes,
        scratch_shapes=[pltpu.VMEM((slabs, rows, LANES), F32),
                        pltpu.VMEM((slabs, 4, rows // 4, LANES), F32)],
        compiler_params=_params("parallel", "parallel"),
        name="in_projection",
    )(x, w_in_p, gq, gkv, wq2, wk, wvt, tables, vone)
    return outs[0], outs[1], outs[2], [outs[3 + 3 * d:6 + 3 * d] for d in range(len(DIL_PAIRS))]


def _mla_kernel(q_ref, k_ref, vt_ref, o_ref, acc_ref, m_ref, st_ref, pt_ref, *, blk):
    nq = q_ref.shape[2] // blk
    strip = 64
    unroll = 8
    m_ref[...] = jnp.full_like(m_ref, -NEG_BIG)
    acc_ref[...] = jnp.zeros_like(acc_ref)

    def rows(i):
        return pl.ds(i * blk if isinstance(i, int) else pl.multiple_of(i * blk, blk), blk)

    half = blk // 2
    assert half % strip == 0

    def half_rows(i, which):
        start = i * blk + which * half
        return pl.ds(start if isinstance(i, int) else pl.multiple_of(start, half), half)

    def scores(item, slot, masked):
        qi, j = item
        for hh in range(2):
            if not masked:
                st_ref[hh, slot] = _dot_nt(k_ref[0, hh, rows(j), :], q_ref[0, hh, rows(qi), :])
            else:
                st_ref[hh, slot, :half, :] = _dot_nt(k_ref[0, hh, half_rows(j, 0), :], q_ref[0, hh, rows(qi), :])
                st_ref[hh, slot, half:, half:] = _dot_nt(k_ref[0, hh, half_rows(j, 1), :],
                                                         q_ref[0, hh, half_rows(qi, 1), :])

    def update(item, slot, masked):
        qi, j = item

        def band(hh, s):
            if not masked:
                return st_ref[hh, slot, s * strip:(s + 1) * strip, :]
            lo = 0 if s * strip < half else half
            tile = st_ref[hh, slot, s * strip:(s + 1) * strip, lo:]
            key = lax.broadcasted_iota(jnp.int32, tile.shape, 0) + s * strip
            qry = lax.broadcasted_iota(jnp.int32, tile.shape, 1) + lo
            return jnp.where(key <= qry, tile, -NEG_BIG)

        def fold(parts):
            top = parts[0]
            for part in parts[1:]:
                top = jnp.maximum(top, part)
            return top

        for hh in range(2):
            tops = [band(hh, s).reshape(strip // 8, 8, -1).max(axis=0) for s in range(blk // strip)]
            if masked:
                n_lo = half // strip
                wide, narrow = fold(tops[:n_lo]), fold(tops[n_lo:])
                top = jnp.concatenate([wide[:, :half], jnp.maximum(wide[:, half:], narrow)], axis=1)
            else:
                top = fold(tops)
            m_old = m_ref[qi, hh]
            m_new = jnp.maximum(m_old, jnp.max(top, axis=0, keepdims=True))
            for s in range(blk // strip):
                tile = band(hh, s)
                lo = blk - tile.shape[1]
                pt_ref[hh, s * strip:(s + 1) * strip, lo:] = jnp.exp2(tile - m_new[:, lo:]).astype(BF16)
            if masked:
                vt = vt_ref[0, hh, j]
                pv = _dot(vt[:, :half], pt_ref[hh, :half, :])
                pv_hi = _dot(vt[:, half:], pt_ref[hh, half:, half:])
                pv = jnp.concatenate([pv[:, :half], pv[:, half:] + pv_hi], axis=1)
            else:
                pv = _dot(vt_ref[0, hh, j], pt_ref[hh])
            acc_ref[qi, hh] = acc_ref[qi, hh] * jnp.exp2(m_old - m_new) + pv
            m_ref[qi, hh] = m_new

    def sweep(items, advance, masked):
        n = len(items)
        peel = (n - 1) % unroll
        scores(items[0], 0, masked)
        for t in range(peel):
            scores(items[t + 1], (t + 1) % 2, masked)
            update(items[t], t % 2, masked)

        def body(_, cur):
            for k in range(unroll):
                nxt = advance(cur)
                scores(nxt, (peel + k + 1) % 2, masked)
                update(cur, (peel + k) % 2, masked)
                cur = nxt
            return cur

        start = (jnp.int32(items[peel][0]), jnp.int32(items[peel][1]))
        lax.fori_loop(0, (n - 1) // unroll, body, start)
        update(items[n - 1], (n - 1) % 2, masked)

    def next_below(item):
        qi, j = item
        wrap = j + 1 >= qi
        return jnp.where(wrap, qi + 1, qi), jnp.where(wrap, 0, j + 1)

    sweep([(qi, j) for qi in range(1, nq) for j in range(qi)], next_below, False)
    sweep([(qi, qi) for qi in range(nq)], lambda item: (item[0] + 1, item[1] + 1), True)
    for qi in range(nq):
        outs = []
        for hh in range(2):
            acc = acc_ref[qi, hh]
            outs.append(acc[:HEAD_DIM] / acc[HEAD_DIM:HEAD_DIM + 1])
        o_ref[0, rows(qi), :] = jnp.concatenate(outs, axis=0).T.astype(o_ref.dtype)


def _mla_attention(q, k, vt):
    B, H, S, _ = q.shape
    blk = vt.shape[-1]
    qk_spec = pl.BlockSpec((1, 2, S, LANES), lambda b, hp: (b, hp, 0, 0))
    vt_spec = pl.BlockSpec((1, 2, S // blk, LANES, blk), lambda b, hp: (b, hp, 0, 0, 0))
    return pl.pallas_call(
        functools.partial(_mla_kernel, blk=blk),
        grid=(B, H // 2),
        in_specs=[qk_spec, qk_spec, vt_spec],
        out_specs=pl.BlockSpec((1, S, LANES), lambda b, hp: (b, 0, hp)),
        out_shape=jax.ShapeDtypeStruct((B, S, H * HEAD_DIM), BF16),
        scratch_shapes=[pltpu.VMEM((S // blk, 2, LANES, blk), F32),
                        pltpu.VMEM((S // blk, 2, 1, blk), F32),
                        pltpu.VMEM((2, 2, blk, blk), F32),
                        pltpu.VMEM((2, blk, blk), BF16)],
        compiler_params=_params("parallel", "parallel"),
        name="mla_attention",
    )(q, k, vt)


def _dil_bias_tiles(bias_ref, dil):
    blk = DIL_BLOCK
    key = lax.broadcasted_iota(jnp.int32, (2 * blk, blk), 0)
    qry = lax.broadcasted_iota(jnp.int32, (2 * blk, blk), 1)
    off = qry + blk - key
    steps = jnp.where((off >= 0) & (off <= blk), off.astype(F32), NEG_BIG)
    for pair in range(DIL_HEADS // 2):
        halves = []
        for hh in range(2):
            slope = 2.0 ** (-8.0 * (2 * pair + hh + 1) / DIL_HEADS)
            halves.append(steps * (-slope * dil * math.log2(math.e)))
        tile = jnp.concatenate(halves, axis=1)
        bias_ref[0, pair] = tile
        bias_ref[1, pair, :blk, :] = jnp.full((blk, 2 * blk), -NEG_BIG, F32)
        bias_ref[1, pair, blk:, :] = tile[blk:]


def _dil_needs_copy(dil, nb):
    return False


def _dil_kernel(q_ref, kc_ref, kp_ref, vc_ref, vp_ref, o_ref, lse_ref, st_scr, o_scr, lse_scr, bias_ref, *bufs,
                dil, lane0):
    blk = DIL_BLOCK
    pairs = DIL_HEADS // 2
    nb = q_ref.shape[2] // blk
    units = dil * nb
    pl.when((pl.program_id(0) == 0) & (pl.program_id(1) == 0))(functools.partial(_dil_bias_tiles, bias_ref, dil))
    if bufs:
        for buf, prev_ref, cur_ref in zip(bufs, (kp_ref, vp_ref), (kc_ref, vc_ref)):
            buf[:, :blk, :] = prev_ref[0]
            buf[:, blk:, :] = cur_ref[0]
    first_span = pl.program_id(1) == 0
    lane = lax.broadcasted_iota(jnp.int32, (blk, LANES), 1)
    first = lane < HEAD_DIM

    def split(u):
        if nb == 1:
            return u, 0
        if isinstance(u, int):
            return u // nb, u % nb
        return lax.shift_right_logical(u, nb.bit_length() - 1), u & (nb - 1)

    def row0(i):
        return i * blk if isinstance(i, int) else pl.multiple_of(i * blk, blk)

    def window(which, r, i, cols):
        cur_ref, prev_ref = ((kc_ref, kp_ref), (vc_ref, vp_ref))[which]
        if bufs:
            return bufs[which][r, pl.ds(row0(i), 2 * blk), cols]
        if isinstance(i, int) and i == 0:
            return jnp.concatenate([prev_ref[0, r, :, cols], cur_ref[0, r, 0:blk, cols]], axis=0)
        return cur_ref[0, r, pl.ds(row0(i - 1), 2 * blk), cols]

    def scores(u, slot):
        r, i = split(u)
        q = q_ref[0, r, pl.ds(row0(i), blk), :]
        no_prev = jnp.logical_and(first_span, i == 0).astype(jnp.int32)
        for pair in range(pairs):
            cols = slice(pair * LANES, (pair + 1) * LANES)
            qp = q[:, cols]
            zero = jnp.zeros_like(qp)
            q2 = jnp.concatenate([jnp.where(first, qp, zero), jnp.where(first, zero, qp)], axis=0)
            st_scr[slot, pair] = _dot_nt(window(0, r, i, cols), q2) + bias_ref[no_prev, pair]

    def finish(u, slot):
        r, i = split(u)
        if dil == 1:
            rows = pl.ds(row0(i), blk)
        else:
            rows = pl.ds(r + i * (blk * dil), blk, stride=dil)
        lse_rows = []
        for pair in range(pairs):
            cols = slice(pair * LANES, (pair + 1) * LANES)
            st = st_scr[slot, pair]
            m = jnp.max(st, axis=0, keepdims=True)
            pt = jnp.exp2(st - m).astype(BF16)
            vt = jnp.concatenate([window(1, r, i, cols).T, jnp.ones((BF16_ROWS, 2 * blk), BF16)], axis=0)
            res = _dot(vt, pt)
            l = res[LANES:LANES + 1, :]
            inv = 1.0 / l
            ot = jnp.concatenate([res[:HEAD_DIM, :LANES] * inv[:, :LANES],
                                  res[HEAD_DIM:LANES, LANES:] * inv[:, LANES:]], axis=0)
            o_scr[pair, rows, :] = ot.T
            lse2 = m + jnp.log2(l)
            lse_rows += [lse2[:, :LANES], lse2[:, LANES:]]
        tile = jnp.concatenate([jnp.zeros((lane0, blk), F32)] * (lane0 > 0) + lse_rows
                               + [jnp.zeros((LANES - lane0 - DIL_HEADS, blk), F32)], axis=0)
        lse_scr[rows, :] = tile.T

    assert units % 2 == 0
    scores(0, 0)
    scores(1, 1)
    finish(0, 0)

    for u in range(1, units - 1):
        scores(u + 1, (u + 1) % 2)
        finish(u, u % 2)
    finish(units - 1, 1)
    for pair in range(pairs):
        o_ref[0, :, pair * LANES:(pair + 1) * LANES] = o_scr[pair].astype(o_ref.dtype)
    lse_ref[0] = lse_scr[...]


def _dil_branch(q, k, v, dil, lane0):
    B, _, M, W = q.shape
    S = M * dil
    span = min(DIL_SPAN, S)
    assert span % (dil * DIL_BLOCK) == 0 and S % span == 0
    rows = span // dil
    nb = rows // DIL_BLOCK
    cur = pl.BlockSpec((1, dil, rows, W), lambda b, c: (b, 0, c, 0))
    prev = pl.BlockSpec((1, dil, DIL_BLOCK, W), lambda b, c: (b, 0, jnp.maximum(c * nb - 1, 0), 0))
    return pl.pallas_call(
        functools.partial(_dil_kernel, dil=dil, lane0=lane0),
        grid=(B, S // span),
        in_specs=[cur, cur, prev, cur, prev],
        out_specs=[pl.BlockSpec((1, span, W), lambda b, c: (b, c, 0)),
                   pl.BlockSpec((1, span, LANES), lambda b, c: (b, c, 0))],
        out_shape=[jax.ShapeDtypeStruct((B, S, W), BF16), jax.ShapeDtypeStruct((B, S, LANES), F32)],
        scratch_shapes=[pltpu.VMEM((2, DIL_HEADS // 2, 2 * DIL_BLOCK, 2 * LANES), F32),
                        pltpu.VMEM((DIL_HEADS // 2, span, LANES), F32),
                        pltpu.VMEM((span, LANES), F32),
                        pltpu.VMEM((2, DIL_HEADS // 2, 2 * DIL_BLOCK, 2 * LANES), F32)]
        + [pltpu.VMEM((dil, DIL_BLOCK + rows, W), BF16)] * (2 if _dil_needs_copy(dil, nb) else 0),
        compiler_params=_params("arbitrary", "arbitrary"),
        name=f"dilated_d{dil}",
    )(q, k, k, v, v)


def _dilated_attention(qkv_per_dil):
    outs, lses = [], []
    for i, ((window, dil), (q, k, v)) in enumerate(zip(DIL_PAIRS, qkv_per_dil)):
        assert window // dil == DIL_BLOCK
        o, lse = _dil_branch(q, k, v, dil, i * DIL_HEADS)
        outs.append(o)
        lses.append(lse)
    return outs, lses


def _layer_norm(y, g, b):
    mu = jnp.mean(y, axis=-1, keepdims=True)
    d = y - mu
    var = jnp.mean(d * d, axis=-1, keepdims=True)
    return d * lax.rsqrt(var + LN_EPS) * g + b


def _out_proj_kernel(om_ref, o0_ref, o1_ref, o2_ref, lse0_ref, lse1_ref, lse2_ref, x_ref, wo_ref, ex_ref,
                     g_ref, b_ref, o_ref):
    H, W = DIL_HEADS, DIL_WIDTH
    l0 = lse0_ref[0] + lse1_ref[0] + lse2_ref[0]
    l1 = pltpu.roll(l0, LANES - H, 1)
    l2 = pltpu.roll(l0, LANES - 2 * H, 1)
    top = jnp.maximum(l0, jnp.maximum(l1, l2))
    e0, e1, e2 = jnp.exp2(l0 - top), jnp.exp2(l1 - top), jnp.exp2(l2 - top)
    inv = 1.0 / (e0 + e1 + e2)
    lane = lax.broadcasted_iota(jnp.int32, l0.shape, 1)
    w = jnp.where(lane < H, e0 * inv,
                  jnp.where(lane < 2 * H, pltpu.roll(e1 * inv, H, 1), pltpu.roll(e2 * inv, 2 * H, 1)))
    wide = _dot(w.astype(BF16), ex_ref[...])
    o_dil = (wide[:, :W] * o0_ref[0] + wide[:, W:2 * W] * o1_ref[0] + wide[:, 2 * W:] * o2_ref[0]).astype(BF16)
    half = om_ref.shape[-1]
    mix = _dot(om_ref[0], wo_ref[:half, :]) + _dot(o_dil, wo_ref[half:, :])
    o_ref[0] = _layer_norm(DN_ALPHA * x_ref[0] + mix, g_ref[...], b_ref[...])


def _out_projection(o_mla, o_dils, lses, x, w_o, expand, g, b):
    B, S, _ = x.shape
    rows = min(PROJ_ROWS, S)
    half_spec = pl.BlockSpec((1, rows, o_mla.shape[-1]), lambda bi, i: (bi, i, 0))
    lse_spec = pl.BlockSpec((1, rows, LANES), lambda bi, i: (bi, i, 0))
    x_spec = pl.BlockSpec((1, rows, D_MODEL), lambda bi, i: (bi, i, 0))
    return pl.pallas_call(
        _out_proj_kernel,
        grid=(B, S // rows),
        in_specs=[half_spec, half_spec, half_spec, half_spec, lse_spec, lse_spec, lse_spec, x_spec,
                  _const_spec(w_o.shape), _const_spec(expand.shape), _const_spec(g.shape), _const_spec(b.shape)],
        out_specs=x_spec,
        out_shape=jax.ShapeDtypeStruct(x.shape, F32),
        compiler_params=_params("parallel", "parallel"),
        name="out_projection_ln",
    )(o_mla, *o_dils, *lses, x, w_o, expand, g, b)


def _ffn_kernel(x_ref, wup_ref, cw_ref, wd_ref, g_ref, b_ref, o_ref, acc_ref, u0_ref, u1_ref, xe_ref, tail_ref):
    rows = x_ref.shape[1]
    pad = tail_ref.shape[2]
    chunks = D_FF // FFN_CHUNK

    def cols(c, part=0):
        start = c * FFN_CHUNK + part * D_FF
        return pl.ds(start if isinstance(c, int) else pl.multiple_of(start, FFN_CHUNK), FFN_CHUNK)

    @pl.when(pl.program_id(1) == 0)
    def _():
        tail_ref[...] = jnp.zeros_like(tail_ref)

    xe_ref[...] = x_ref[0].astype(BF16)
    acc_ref[...] = jnp.zeros_like(acc_ref)

    def up(c, u_ref):
        for part in range(2):
            u = _dot(xe_ref[...], wup_ref[:, cols(c, part)])
            u_ref[part, :pad, :] = tail_ref[c, part]
            u_ref[part, pad:, :] = u
            tail_ref[c, part] = u[rows - pad:, :]

    def conv(u_ref, part, taps):
        y = taps[3:4, :]
        for j in range(CONV_WIDTH):
            shift = CONV_WIDTH - 1 - j
            y = y + taps[j:j + 1, :] * u_ref[part, pl.ds(pad - shift, rows), :]
        return y

    def down(c, u_ref):
        ya = conv(u_ref, 0, cw_ref[:, cols(c, 0)])
        yg = conv(u_ref, 1, cw_ref[:, cols(c, 1)])
        c1 = math.sqrt(2.0 / math.pi)
        th = jnp.tanh(yg * (c1 + (c1 * 0.044715) * (yg * yg)))
        hidden = ((yg + yg * th) * ya).astype(BF16)
        acc_ref[...] += _dot(hidden, wd_ref[cols(c), :])

    assert chunks % 2 == 1
    up(0, u0_ref)

    def body(j, carry):
        c = 2 * j
        up(c + 1, u1_ref)
        down(c, u0_ref)
        up(c + 2, u0_ref)
        down(c + 1, u1_ref)
        return carry

    lax.fori_loop(0, chunks // 2, body, 0)
    down(chunks - 1, u0_ref)
    o_ref[0] = _layer_norm(DN_ALPHA * x_ref[0] + acc_ref[...], g_ref[...], b_ref[...])


def _ffn(x1, w_up, cw, wd, g, b):
    B, S, _ = x1.shape
    rows = min(FFN_ROWS, S)
    pad = F32_ROWS
    x_spec = pl.BlockSpec((1, rows, D_MODEL), lambda bi, i: (bi, i, 0))
    return pl.pallas_call(
        _ffn_kernel,
        grid=(B, S // rows),
        in_specs=[x_spec, _const_spec(w_up.shape), _const_spec(cw.shape),
                  _const_spec(wd.shape), _const_spec(g.shape), _const_spec(b.shape)],
        out_specs=x_spec,
        out_shape=jax.ShapeDtypeStruct(x1.shape, F32),
        scratch_shapes=[pltpu.VMEM((rows, D_MODEL), F32),
                        pltpu.VMEM((2, pad + rows, FFN_CHUNK), F32), pltpu.VMEM((2, pad + rows, FFN_CHUNK), F32),
                        pltpu.VMEM((rows, D_MODEL), BF16),
                        pltpu.VMEM((D_FF // FFN_CHUNK, 2, pad, FFN_CHUNK), F32)],
        compiler_params=_params("parallel", "arbitrary"),
        name="conv_ffn_ln",
    )(x1, w_up, cw, wd, g, b)


def _pad_cols(w, width):
    return jnp.pad(w, ((0, 0), (0, width - w.shape[1])))


def _head_groups(w):
    rank, heads, e = w.shape
    return jnp.pad(w, ((0, 0), (0, 0), (0, LANES - e))).reshape(rank, heads * LANES)


def _prepare(w_in, g_cq, g_ckv, w_uq, w_uk, w_uv, w_o, ln1_g, ln1_b, w_up, conv_w, conv_b, w_down, ln2_g, ln2_b):
    r0, r1, r2 = MLA_Q_RANK, MLA_Q_RANK + MLA_KV_RANK, MLA_Q_RANK + MLA_KV_RANK + MLA_ROPE_DIM
    w_in_p = jnp.concatenate([w_in[:, :r1], _pad_cols(w_in[:, r1:r2], LANES), w_in[:, r2:]], axis=1).astype(BF16)

    half = MLA_ROPE_DIM // 2
    rope = w_uq[:, :, MLA_NOPE_DIM:]
    swapped = jnp.concatenate([-rope[:, :, half:], rope[:, :, :half]], axis=-1)
    swapped = jnp.concatenate([jnp.zeros_like(w_uq[:, :, :MLA_NOPE_DIM]), swapped], axis=-1)
    wq2 = jnp.concatenate([_head_groups(w_uq), _head_groups(swapped)], axis=1).astype(BF16)

    wk = _head_groups(w_uk).astype(BF16)
    wvt = jnp.pad(w_uv.transpose(1, 2, 0), ((0, 0), (0, LANES - HEAD_DIM), (0, 0))).astype(BF16)
    vone = jnp.zeros((LANES, 1), F32).at[HEAD_DIM, 0].set(1.0)

    assert D_FF % FFN_CHUNK == 0
    taps = jnp.concatenate([conv_w, conv_b[None, :]], axis=0)
    cw = jnp.concatenate([0.5 * taps[:, :D_FF], taps[:, D_FF:]], axis=1)
    src = jnp.arange(LANES)[:, None]
    dst = jnp.arange(len(DIL_PAIRS) * DIL_WIDTH)[None, :]
    expand = (src == dst // HEAD_DIM).astype(BF16)
    row = lambda a: a.reshape(1, -1)
    return dict(w_in_p=w_in_p, gq=row(g_cq), gkv=row(g_ckv), wq2=wq2, wk=wk, wvt=wvt, vone=vone,
                w_o=w_o.astype(BF16), expand=expand, ln1=(row(ln1_g), row(ln1_b)), w_up=w_up.astype(BF16), cw=cw,
                wd=w_down.astype(BF16),
                ln2=(row(ln2_g), row(ln2_b)))


def _freq_lanes():
    half = MLA_ROPE_DIM // 2
    freqs = ROPE_THETA ** (-jnp.arange(half, dtype=F32) / half)
    zeros = jnp.zeros((MLA_NOPE_DIM,), F32)
    return jnp.concatenate([zeros, freqs, freqs, jnp.zeros((LANES - MLA_NOPE_DIM - MLA_ROPE_DIM,), F32)])[None, :]


def kernel(x, w_in, g_cq, g_ckv, w_uq, w_uk, w_uv, w_o, ln1_g, ln1_b, w_up, conv_w, conv_b, w_down, ln2_g, ln2_b):
    B, S, _ = x.shape
    p = _prepare(w_in, g_cq, g_ckv, w_uq, w_uk, w_uv, w_o, ln1_g, ln1_b, w_up, conv_w, conv_b, w_down, ln2_g, ln2_b)
    scale = math.log2(math.e) / math.sqrt(MLA_NOPE_DIM + MLA_ROPE_DIM)
    tables = _rope_tables(S, _freq_lanes(), scale)
    q, k, vt, qkv_dil = _projection(x, p["w_in_p"], p["gq"], p["gkv"], p["wq2"], p["wk"], p["wvt"], tables,
                                    p["vone"])
    o_mla = _mla_attention(q, k, vt)
    o_dils, lses = _dilated_attention(qkv_dil)
    x1 = _out_projection(o_mla, o_dils, lses, x, p["w_o"], p["expand"], *p["ln1"])
    return _ffn(x1, p["w_up"], p["cw"], p["wd"], *p["ln2"])
```

```python
import functools
import math

import jax
import jax.numpy as jnp
from jax import lax
from jax.experimental import pallas as pl
from jax.experimental.pallas import tpu as pltpu

D_MODEL = 1024
HEAD_DIM = 64
MLA_HEADS = 8
MLA_Q_RANK = 256
MLA_KV_RANK = 128
MLA_NOPE_DIM = 64
MLA_ROPE_DIM = 32
ROPE_THETA = 10000.0
DIL_HEADS = 8
DIL_PAIRS = ((128, 1), (512, 4), (2048, 16))
DIL_BLOCK = 128
DIL_WIDTH = DIL_HEADS * HEAD_DIM
D_FF = 2816
CONV_WIDTH = 3
DEPTH = 1
DN_ALPHA = (2.0 * DEPTH) ** 0.25
LN_EPS = 1e-5
RMS_EPS = 1e-6

LANES = 128
BF16_ROWS = 16
F32_ROWS = 8
VMEM_LIMIT = 56 * 1024 * 1024

PROJ_ROWS = 512
OUT_ROWS = 1024
DIL_SPAN = 2048
FFN_ROWS = 1024
FFN_CHUNK = 256
NEG_BIG = 1e30

BF16 = jnp.bfloat16
F32 = jnp.float32


def _dot(a, b):
    return jnp.dot(a, b, preferred_element_type=F32)


def _dot_nt(a, b):
    return lax.dot_general(a, b, (((1,), (1,)), ((), ())), preferred_element_type=F32)


def _params(*sem, flags=None):
    return pltpu.CompilerParams(dimension_semantics=sem, vmem_limit_bytes=VMEM_LIMIT, flags=flags)


def _const_spec(shape):
    zeros = (0,) * len(shape)
    return pl.BlockSpec(shape, lambda *_: zeros, pipeline_mode=pl.Buffered(1))


def _rope_table_kernel(freq_ref, out_ref, *, scale):
    rows = out_ref.shape[1]
    pos = lax.broadcasted_iota(jnp.int32, (rows, LANES), 0).astype(F32)
    ang = pos * freq_ref[...]
    c = jnp.cos(ang)
    s = jnp.sin(ang)
    out_ref[0] = c
    out_ref[1] = s
    out_ref[2] = c * scale
    out_ref[3] = s * scale


def _rope_tables(seq, freq_lanes, scale):
    return pl.pallas_call(
        functools.partial(_rope_table_kernel, scale=scale),
        out_shape=jax.ShapeDtypeStruct((4, seq, LANES), F32),
        name="rope_tables",
    )(freq_lanes)


def _rms(x, g):
    ms = jnp.mean(x * x, axis=-1, keepdims=True)
    return x * lax.rsqrt(ms + RMS_EPS) * g


def _proj_kernel(x_ref, w_in_ref, gq_ref, gkv_ref, wq_ref, wk_ref, wvt_ref, tab_ref, vone_ref,
                 q_ref, k_ref, vt_ref, *rest):
    dil_refs, stage_refs = rest[:-2], rest[-2:]
    rows = x_ref.shape[1]
    xb = x_ref[0].astype(BF16)
    h = _dot(xb, w_in_ref[...])
    cq = _rms(h[:, :MLA_Q_RANK], gq_ref[...]).astype(BF16)
    ckv = _rms(h[:, MLA_Q_RANK:MLA_Q_RANK + MLA_KV_RANK], gkv_ref[...]).astype(BF16)
    kr = h[:, 3 * LANES:4 * LANES]
    cos, sin, cos_q, sin_q = tab_ref[0], tab_ref[1], tab_ref[2], tab_ref[3]

    k_plain = pltpu.roll(kr, 64, 1)
    k_swap = pltpu.roll(kr, 80, 1) - pltpu.roll(kr, 48, 1)
    k_rope = k_plain * cos + k_swap * sin

    q2 = _dot(cq, wq_ref[...])
    kn = _dot(ckv, wk_ref[...])
    vt = _dot_nt(wvt_ref[...].reshape(MLA_HEADS * LANES, MLA_KV_RANK), ckv)
    hw = MLA_HEADS * LANES
    for hd in range(MLA_HEADS):
        lo = hd * LANES
        qh = q2[:, lo:lo + LANES] * cos_q + q2[:, hw + lo:hw + lo + LANES] * sin_q
        q_ref[0, hd] = qh.astype(BF16)
        k_ref[0, hd] = (kn[:, lo:lo + LANES] + k_rope).astype(BF16)
        vt_ref[0, hd, 0] = (vt[lo:lo + LANES, :] + vone_ref[...]).astype(BF16)

    base = 4 * LANES
    groups = DIL_WIDTH // LANES
    assert [d for _, d in DIL_PAIRS] == [1, 4, 16]
    flat_ref, by4_ref = stage_refs
    for c in range(3 * groups):
        slab = h[:, base + c * LANES:base + (c + 1) * LANES]
        slab = slab * (math.log2(math.e) / math.sqrt(HEAD_DIM)) if c < groups else slab
        lanes = slice((c % groups) * LANES, (c % groups + 1) * LANES)
        flat_ref[c] = slab
        dil_refs[c // groups][0, 0, :, lanes] = slab.astype(BF16)
        for r4 in range(4):
            piece = flat_ref[c, pl.ds(r4, rows // 4, stride=4), :]
            by4_ref[c, r4] = piece
            dil_refs[3 + c // groups][0, r4, :, lanes] = piece.astype(BF16)
            for r in range(4):
                piece16 = by4_ref[c, r4, pl.ds(r, rows // 16, stride=4), :]
                dil_refs[6 + c // groups][0, r4 + 4 * r, :, lanes] = piece16.astype(BF16)


def _projection(x, w_in_p, gq, gkv, wq2, wk, wvt, tables, vone):
    B, S, _ = x.shape
    rows = min(PROJ_ROWS, S)
    head_shape = jax.ShapeDtypeStruct((B, MLA_HEADS, S, LANES), BF16)
    vt_shape = jax.ShapeDtypeStruct((B, MLA_HEADS, S // rows, LANES, rows), BF16)
    head_spec = pl.BlockSpec((1, MLA_HEADS, rows, LANES), lambda b, i: (b, 0, i, 0))
    vt_spec = pl.BlockSpec((1, MLA_HEADS, 1, LANES, rows), lambda b, i: (b, 0, i, 0, 0))
    dil_shapes, dil_specs = [], []
    for _, dil in DIL_PAIRS:
        assert rows % (dil * BF16_ROWS) == 0
        dil_shapes += [jax.ShapeDtypeStruct((B, dil, S // dil, DIL_WIDTH), BF16)] * 3
        dil_specs += [pl.BlockSpec((1, dil, rows // dil, DIL_WIDTH), lambda b, i: (b, 0, i, 0))] * 3
    slabs = 3 * DIL_WIDTH // LANES
    outs = pl.pallas_call(
        _proj_kernel,
        grid=(B, S // rows),
        in_specs=[
            pl.BlockSpec((1, rows, D_MODEL), lambda b, i: (b, i, 0)),
            _const_spec(w_in_p.shape),
            _const_spec(gq.shape),
            _const_spec(gkv.shape),
            _const_spec(wq2.shape),
            _const_spec(wk.shape),
            _const_spec(wvt.shape),
            pl.BlockSpec((4, rows, LANES), lambda b, i: (0, i, 0)),
            _const_spec(vone.shape),
        ],
        out_specs=[head_spec, head_spec, vt_spec] + dil_specs,
        out_shape=[head_shape, head_shape, vt_shape] + dil_shapes,
        scratch_shapes=[pltpu.VMEM((slabs, rows, LANES), F32),
                        pltpu.VMEM((slabs, 4, rows // 4, LANES), F32)],
        compiler_params=_params("parallel", "parallel"),
        name="in_projection",
    )(x, w_in_p, gq, gkv, wq2, wk, wvt, tables, vone)
    return outs[0], outs[1], outs[2], [outs[3 + 3 * d:6 + 3 * d] for d in range(len(DIL_PAIRS))]


def _mla_kernel(q_ref, k_ref, vt_ref, o_ref, acc_ref, m_ref, st_ref, pt_ref, *, blk):
    nq = q_ref.shape[2] // blk
    strip = 64
    unroll = 8
    m_ref[...] = jnp.full_like(m_ref, -NEG_BIG)
    acc_ref[...] = jnp.zeros_like(acc_ref)

    def rows(i):
        return pl.ds(i * blk if isinstance(i, int) else pl.multiple_of(i * blk, blk), blk)

    half = blk // 2
    assert half % strip == 0

    def half_rows(i, which):
        start = i * blk + which * half
        return pl.ds(start if isinstance(i, int) else pl.multiple_of(start, half), half)

    def scores(item, slot, masked):
        qi, j = item
        for hh in range(2):
            if not masked:
                st_ref[hh, slot] = _dot_nt(k_ref[0, hh, rows(j), :], q_ref[0, hh, rows(qi), :])
            else:
                st_ref[hh, slot, :half, :] = _dot_nt(k_ref[0, hh, half_rows(j, 0), :], q_ref[0, hh, rows(qi), :])
                st_ref[hh, slot, half:, half:] = _dot_nt(k_ref[0, hh, half_rows(j, 1), :],
                                                         q_ref[0, hh, half_rows(qi, 1), :])

    def update(item, slot, masked):
        qi, j = item

        def band(hh, s):
            if not masked:
                return st_ref[hh, slot, s * strip:(s + 1) * strip, :]
            lo = 0 if s * strip < half else half
            tile = st_ref[hh, slot, s * strip:(s + 1) * strip, lo:]
            key = lax.broadcasted_iota(jnp.int32, tile.shape, 0) + s * strip
            qry = lax.broadcasted_iota(jnp.int32, tile.shape, 1) + lo
            return jnp.where(key <= qry, tile, -NEG_BIG)

        def fold(parts):
            top = parts[0]
            for part in parts[1:]:
                top = jnp.maximum(top, part)
            return top

        for hh in range(2):
            tops = [band(hh, s).reshape(strip // 8, 8, -1).max(axis=0) for s in range(blk // strip)]
            if masked:
                n_lo = half // strip
                wide, narrow = fold(tops[:n_lo]), fold(tops[n_lo:])
                top = jnp.concatenate([wide[:, :half], jnp.maximum(wide[:, half:], narrow)], axis=1)
            else:
                top = fold(tops)
            m_old = m_ref[qi, hh]
            m_new = jnp.maximum(m_old, jnp.max(top, axis=0, keepdims=True))
            for s in range(blk // strip):
                tile = band(hh, s)
                lo = blk - tile.shape[1]
                pt_ref[hh, s * strip:(s + 1) * strip, lo:] = jnp.exp2(tile - m_new[:, lo:]).astype(BF16)
            if masked:
                vt = vt_ref[0, hh, j]
                pv = _dot(vt[:, :half], pt_ref[hh, :half, :])
                pv_hi = _dot(vt[:, half:], pt_ref[hh, half:, half:])
                pv = jnp.concatenate([pv[:, :half], pv[:, half:] + pv_hi], axis=1)
            else:
                pv = _dot(vt_ref[0, hh, j], pt_ref[hh])
            acc_ref[qi, hh] = acc_ref[qi, hh] * jnp.exp2(m_old - m_new) + pv
            m_ref[qi, hh] = m_new

    def sweep(items, advance, masked):
        n = len(items)
        peel = (n - 1) % unroll if n - 1 > unroll else n - 1
        scores(items[0], 0, masked)
        for t in range(peel):
            scores(items[t + 1], (t + 1) % 2, masked)
            update(items[t], t % 2, masked)

        def body(_, cur):
            for k in range(unroll):
                nxt = advance(cur)
                scores(nxt, (peel + k + 1) % 2, masked)
                update(cur, (peel + k) % 2, masked)
                cur = nxt
            return cur

        if peel < n - 1:
            start = (jnp.int32(items[peel][0]), jnp.int32(items[peel][1]))
            lax.fori_loop(0, (n - 1 - peel) // unroll, body, start)
        update(items[n - 1], (n - 1) % 2, masked)

    def next_below(item):
        qi, j = item
        wrap = j + 1 >= qi
        return jnp.where(wrap, qi + 1, qi), jnp.where(wrap, 0, j + 1)

    sweep([(qi, j) for qi in range(1, nq) for j in range(qi)], next_below, False)
    sweep([(qi, qi) for qi in range(nq)], lambda item: (item[0] + 1, item[1] + 1), True)
    for qi in range(nq):
        outs = []
        for hh in range(2):
            acc = acc_ref[qi, hh]
            outs.append(acc[:HEAD_DIM] / acc[HEAD_DIM:HEAD_DIM + 1])
        o_ref[0, rows(qi), :] = jnp.concatenate(outs, axis=0).T.astype(o_ref.dtype)


def _mla_attention(q, k, vt):
    B, H, S, _ = q.shape
    blk = vt.shape[-1]
    qk_spec = pl.BlockSpec((1, 2, S, LANES), lambda b, hp: (b, hp, 0, 0))
    vt_spec = pl.BlockSpec((1, 2, S // blk, LANES, blk), lambda b, hp: (b, hp, 0, 0, 0))
    return pl.pallas_call(
        functools.partial(_mla_kernel, blk=blk),
        grid=(B, H // 2),
        in_specs=[qk_spec, qk_spec, vt_spec],
        out_specs=pl.BlockSpec((1, S, LANES), lambda b, hp: (b, 0, hp)),
        out_shape=jax.ShapeDtypeStruct((B, S, H * HEAD_DIM), BF16),
        scratch_shapes=[pltpu.VMEM((S // blk, 2, LANES, blk), F32),
                        pltpu.VMEM((S // blk, 2, 1, blk), F32),
                        pltpu.VMEM((2, 2, blk, blk), F32),
                        pltpu.VMEM((2, blk, blk), BF16)],
        compiler_params=_params("parallel", "parallel"),
        name="mla_attention",
    )(q, k, vt)


def _dil_bias_tiles(bias_ref, dil):
    blk = DIL_BLOCK
    key = lax.broadcasted_iota(jnp.int32, (2 * blk, blk), 0)
    qry = lax.broadcasted_iota(jnp.int32, (2 * blk, blk), 1)
    off = qry + blk - key
    steps = jnp.where((off >= 0) & (off <= blk), off.astype(F32), NEG_BIG)
    for pair in range(DIL_HEADS // 2):
        halves = []
        for hh in range(2):
            slope = 2.0 ** (-8.0 * (2 * pair + hh + 1) / DIL_HEADS)
            halves.append(steps * (-slope * dil * math.log2(math.e)))
        tile = jnp.concatenate(halves, axis=1)
        bias_ref[0, pair] = tile
        bias_ref[1, pair, :blk, :] = jnp.full((blk, 2 * blk), -NEG_BIG, F32)
        bias_ref[1, pair, blk:, :] = tile[blk:]


def _dil_kernel(q_ref, kc_ref, kp_ref, vc_ref, vp_ref, o_ref, lse_ref, st_scr, o_scr, lse_scr, bias_ref, *,
                dil, lane0):
    blk = DIL_BLOCK
    pairs = DIL_HEADS // 2
    nb = q_ref.shape[2] // blk
    units = dil * nb
    pl.when((pl.program_id(0) == 0) & (pl.program_id(1) == 0))(functools.partial(_dil_bias_tiles, bias_ref, dil))
    first_span = pl.program_id(1) == 0
    lane = lax.broadcasted_iota(jnp.int32, (blk, LANES), 1)
    first = lane < HEAD_DIM

    def split(u):
        return u // nb, u % nb

    def row0(i):
        return i * blk

    def window(which, r, i, cols):
        cur_ref, prev_ref = ((kc_ref, kp_ref), (vc_ref, vp_ref))[which]
        if i == 0:
            return jnp.concatenate([prev_ref[0, r, :, cols], cur_ref[0, r, 0:blk, cols]], axis=0)
        return cur_ref[0, r, pl.ds(row0(i - 1), 2 * blk), cols]

    def scores(u, slot):
        r, i = split(u)
        q = q_ref[0, r, pl.ds(row0(i), blk), :]
        no_prev = first_span.astype(jnp.int32) if i == 0 else 0
        for pair in range(pairs):
            cols = slice(pair * LANES, (pair + 1) * LANES)
            qp = q[:, cols]
            zero = jnp.zeros_like(qp)
            q2 = jnp.concatenate([jnp.where(first, qp, zero), jnp.where(first, zero, qp)], axis=0)
            st_scr[slot, pair] = _dot_nt(window(0, r, i, cols), q2) + bias_ref[no_prev, pair]

    def finish(u, slot):
        r, i = split(u)
        if dil == 1:
            rows = pl.ds(row0(i), blk)
        else:
            rows = pl.ds(r + i * (blk * dil), blk, stride=dil)
        lse_rows = []
        for pair in range(pairs):
            cols = slice(pair * LANES, (pair + 1) * LANES)
            st = st_scr[slot, pair]
            m = jnp.max(st, axis=0, keepdims=True)
            pt = jnp.exp2(st - m).astype(BF16)
            vt = jnp.concatenate([window(1, r, i, cols).T, jnp.ones((BF16_ROWS, 2 * blk), BF16)], axis=0)
            res = _dot(vt, pt)
            l = res[LANES:LANES + 1, :]
            inv = 1.0 / l
            ot = jnp.concatenate([res[:HEAD_DIM, :LANES] * inv[:, :LANES],
                                  res[HEAD_DIM:LANES, LANES:] * inv[:, LANES:]], axis=0)
            o_scr[pair, rows, :] = ot.T
            lse2 = m + jnp.log2(l)
            lse_rows += [lse2[:, :LANES], lse2[:, LANES:]]
        tile = jnp.concatenate([jnp.zeros((lane0, blk), F32)] * (lane0 > 0) + lse_rows
                               + [jnp.zeros((LANES - lane0 - DIL_HEADS, blk), F32)], axis=0)
        lse_scr[rows, :] = tile.T

    scores(0, 0)
    for u in range(units):
        if u + 1 < units:
            scores(u + 1, (u + 1) % 2)
        finish(u, u % 2)
    for pair in range(pairs):
        o_ref[0, :, pair * LANES:(pair + 1) * LANES] = o_scr[pair].astype(o_ref.dtype)
    lse_ref[0] = lse_scr[...]


def _dil_branch(q, k, v, dil, lane0):
    B, _, M, W = q.shape
    S = M * dil
    span = min(DIL_SPAN, S)
    assert span % (dil * DIL_BLOCK) == 0 and S % span == 0
    rows = span // dil
    nb = rows // DIL_BLOCK
    cur = pl.BlockSpec((1, dil, rows, W), lambda b, c: (b, 0, c, 0))
    prev = pl.BlockSpec((1, dil, DIL_BLOCK, W), lambda b, c: (b, 0, jnp.maximum(c * nb - 1, 0), 0))
    return pl.pallas_call(
        functools.partial(_dil_kernel, dil=dil, lane0=lane0),
        grid=(B, S // span),
        in_specs=[cur, cur, prev, cur, prev],
        out_specs=[pl.BlockSpec((1, span, W), lambda b, c: (b, c, 0)),
                   pl.BlockSpec((1, span, LANES), lambda b, c: (b, c, 0))],
        out_shape=[jax.ShapeDtypeStruct((B, S, W), BF16), jax.ShapeDtypeStruct((B, S, LANES), F32)],
        scratch_shapes=[pltpu.VMEM((2, DIL_HEADS // 2, 2 * DIL_BLOCK, 2 * LANES), F32),
                        pltpu.VMEM((DIL_HEADS // 2, span, LANES), F32),
                        pltpu.VMEM((span, LANES), F32),
                        pltpu.VMEM((2, DIL_HEADS // 2, 2 * DIL_BLOCK, 2 * LANES), F32)],
        compiler_params=_params("arbitrary", "arbitrary"),
        name=f"dilated_d{dil}",
    )(q, k, k, v, v)


def _dilated_attention(qkv_per_dil):
    outs, lses = [], []
    for i, ((window, dil), (q, k, v)) in enumerate(zip(DIL_PAIRS, qkv_per_dil)):
        assert window // dil == DIL_BLOCK
        o, lse = _dil_branch(q, k, v, dil, i * DIL_HEADS)
        outs.append(o)
        lses.append(lse)
    return outs, lses


def _layer_norm(y, g, b):
    mu = jnp.mean(y, axis=-1, keepdims=True)
    d = y - mu
    var = jnp.mean(d * d, axis=-1, keepdims=True)
    return d * lax.rsqrt(var + LN_EPS) * g + b


def _out_proj_kernel(om_ref, o0_ref, o1_ref, o2_ref, lse0_ref, lse1_ref, lse2_ref, x_ref, wo_ref, ex_ref,
                     g_ref, b_ref, o_ref):
    H, W = DIL_HEADS, DIL_WIDTH
    l0 = lse0_ref[0] + lse1_ref[0] + lse2_ref[0]
    l1 = pltpu.roll(l0, LANES - H, 1)
    l2 = pltpu.roll(l0, LANES - 2 * H, 1)
    top = jnp.maximum(l0, jnp.maximum(l1, l2))
    e0, e1, e2 = jnp.exp2(l0 - top), jnp.exp2(l1 - top), jnp.exp2(l2 - top)
    inv = 1.0 / (e0 + e1 + e2)
    lane = lax.broadcasted_iota(jnp.int32, l0.shape, 1)
    w = jnp.where(lane < H, e0 * inv,
                  jnp.where(lane < 2 * H, pltpu.roll(e1 * inv, H, 1), pltpu.roll(e2 * inv, 2 * H, 1)))
    wide = _dot(w.astype(BF16), ex_ref[...])
    o_dil = (wide[:, :W] * o0_ref[0] + wide[:, W:2 * W] * o1_ref[0] + wide[:, 2 * W:] * o2_ref[0]).astype(BF16)
    half = om_ref.shape[-1]
    mix = _dot(om_ref[0], wo_ref[:half, :]) + _dot(o_dil, wo_ref[half:, :])
    o_ref[0] = _layer_norm(DN_ALPHA * x_ref[0] + mix, g_ref[...], b_ref[...])


def _out_projection(o_mla, o_dils, lses, x, w_o, expand, g, b):
    B, S, _ = x.shape
    rows = min(OUT_ROWS, S)
    half_spec = pl.BlockSpec((1, rows, o_mla.shape[-1]), lambda bi, i: (bi, i, 0))
    lse_spec = pl.BlockSpec((1, rows, LANES), lambda bi, i: (bi, i, 0))
    x_spec = pl.BlockSpec((1, rows, D_MODEL), lambda bi, i: (bi, i, 0))
    return pl.pallas_call(
        _out_proj_kernel,
        grid=(B, S // rows),
        in_specs=[half_spec, half_spec, half_spec, half_spec, lse_spec, lse_spec, lse_spec, x_spec,
                  _const_spec(w_o.shape), _const_spec(expand.shape), _const_spec(g.shape), _const_spec(b.shape)],
        out_specs=x_spec,
        out_shape=jax.ShapeDtypeStruct(x.shape, F32),
        compiler_params=_params("parallel", "parallel"),
        name="out_projection_ln",
    )(o_mla, *o_dils, *lses, x, w_o, expand, g, b)


def _ffn_kernel(x_ref, wup_ref, cw_ref, wd_ref, g_ref, b_ref, o_ref, acc_ref, u0_ref, u1_ref, xe_ref, tail_ref):
    rows = x_ref.shape[1]
    pad = tail_ref.shape[2]
    chunks = D_FF // FFN_CHUNK

    def cols(c, part=0):
        start = c * FFN_CHUNK + part * D_FF
        return pl.ds(start if isinstance(c, int) else pl.multiple_of(start, FFN_CHUNK), FFN_CHUNK)

    @pl.when(pl.program_id(1) == 0)
    def _():
        tail_ref[...] = jnp.zeros_like(tail_ref)

    xe_ref[...] = x_ref[0].astype(BF16)
    acc_ref[...] = jnp.zeros_like(acc_ref)

    def up(c, u_ref):
        for part in range(2):
            u = _dot(xe_ref[...], wup_ref[:, cols(c, part)])
            u_ref[part, :pad, :] = tail_ref[c, part]
            u_ref[part, pad:, :] = u
            tail_ref[c, part] = u[rows - pad:, :]

    def conv(u_ref, part, taps):
        y = taps[3:4, :]
        for j in range(CONV_WIDTH):
            shift = CONV_WIDTH - 1 - j
            y = y + taps[j:j + 1, :] * u_ref[part, pl.ds(pad - shift, rows), :]
        return y

    def down(c, u_ref):
        ya = conv(u_ref, 0, cw_ref[:, cols(c, 0)])
        yg = conv(u_ref, 1, cw_ref[:, cols(c, 1)])
        c1 = math.sqrt(2.0 / math.pi)
        th = jnp.tanh(yg * (c1 + (c1 * 0.044715) * (yg * yg)))
        hidden = ((yg + yg * th) * ya).astype(BF16)
        acc_ref[...] += _dot(hidden, wd_ref[cols(c), :])

    assert chunks % 2 == 1
    up(0, u0_ref)

    def body(j, carry):
        c = 2 * j
        up(c + 1, u1_ref)
        down(c, u0_ref)
        up(c + 2, u0_ref)
        down(c + 1, u1_ref)
        return carry

    lax.fori_loop(0, chunks // 2, body, 0)
    down(chunks - 1, u0_ref)
    o_ref[0] = _layer_norm(DN_ALPHA * x_ref[0] + acc_ref[...], g_ref[...], b_ref[...])


def _ffn(x1, w_up, cw, wd, g, b):
    B, S, _ = x1.shape
    rows = min(FFN_ROWS, S)
    pad = F32_ROWS
    x_spec = pl.BlockSpec((1, rows, D_MODEL), lambda bi, i: (bi, i, 0))
    return pl.pallas_call(
        _ffn_kernel,
        grid=(B, S // rows),
        in_specs=[x_spec, _const_spec(w_up.shape), _const_spec(cw.shape),
                  _const_spec(wd.shape), _const_spec(g.shape), _const_spec(b.shape)],
        out_specs=x_spec,
        out_shape=jax.ShapeDtypeStruct(x1.shape, F32),
        scratch_shapes=[pltpu.VMEM((rows, D_MODEL), F32),
                        pltpu.VMEM((2, pad + rows, FFN_CHUNK), F32), pltpu.VMEM((2, pad + rows, FFN_CHUNK), F32),
                        pltpu.VMEM((rows, D_MODEL), BF16),
                        pltpu.VMEM((D_FF // FFN_CHUNK, 2, pad, FFN_CHUNK), F32)],
        compiler_params=_params("parallel", "arbitrary"),
        name="conv_ffn_ln",
    )(x1, w_up, cw, wd, g, b)


def _pad_cols(w, width):
    return jnp.pad(w, ((0, 0), (0, width - w.shape[1])))


def _head_groups(w):
    rank, heads, e = w.shape
    return jnp.pad(w, ((0, 0), (0, 0), (0, LANES - e))).reshape(rank, heads * LANES)


def _prepare(w_in, g_cq, g_ckv, w_uq, w_uk, w_uv, w_o, ln1_g, ln1_b, w_up, conv_w, conv_b, w_down, ln2_g, ln2_b):
    r0, r1, r2 = MLA_Q_RANK, MLA_Q_RANK + MLA_KV_RANK, MLA_Q_RANK + MLA_KV_RANK + MLA_ROPE_DIM
    w_in_p = jnp.concatenate([w_in[:, :r1], _pad_cols(w_in[:, r1:r2], LANES), w_in[:, r2:]], axis=1).astype(BF16)

    half = MLA_ROPE_DIM // 2
    rope = w_uq[:, :, MLA_NOPE_DIM:]
    swapped = jnp.concatenate([-rope[:, :, half:], rope[:, :, :half]], axis=-1)
    swapped = jnp.concatenate([jnp.zeros_like(w_uq[:, :, :MLA_NOPE_DIM]), swapped], axis=-1)
    wq2 = jnp.concatenate([_head_groups(w_uq), _head_groups(swapped)], axis=1).astype(BF16)

    wk = _head_groups(w_uk).astype(BF16)
    wvt = jnp.pad(w_uv.transpose(1, 2, 0), ((0, 0), (0, LANES - HEAD_DIM), (0, 0))).astype(BF16)
    vone = jnp.zeros((LANES, 1), F32).at[HEAD_DIM, 0].set(1.0)

    assert D_FF % FFN_CHUNK == 0
    taps = jnp.concatenate([conv_w, conv_b[None, :]], axis=0)
    cw = jnp.concatenate([0.5 * taps[:, :D_FF], taps[:, D_FF:]], axis=1)
    src = jnp.arange(LANES)[:, None]
    dst = jnp.arange(len(DIL_PAIRS) * DIL_WIDTH)[None, :]
    expand = (src == dst // HEAD_DIM).astype(BF16)
    row = lambda a: a.reshape(1, -1)
    return dict(w_in_p=w_in_p, gq=row(g_cq), gkv=row(g_ckv), wq2=wq2, wk=wk, wvt=wvt, vone=vone,
                w_o=w_o.astype(BF16), expand=expand, ln1=(row(ln1_g), row(ln1_b)), w_up=w_up.astype(BF16), cw=cw,
                wd=w_down.astype(BF16),
                ln2=(row(ln2_g), row(ln2_b)))


def _freq_lanes():
    half = MLA_ROPE_DIM // 2
    freqs = ROPE_THETA ** (-jnp.arange(half, dtype=F32) / half)
    zeros = jnp.zeros((MLA_NOPE_DIM,), F32)
    return jnp.concatenate([zeros, freqs, freqs, jnp.zeros((LANES - MLA_NOPE_DIM - MLA_ROPE_DIM,), F32)])[None, :]


def kernel(x, w_in, g_cq, g_ckv, w_uq, w_uk, w_uv, w_o, ln1_g, ln1_b, w_up, conv_w, conv_b, w_down, ln2_g, ln2_b):
    B, S, _ = x.shape
    p = _prepare(w_in, g_cq, g_ckv, w_uq, w_uk, w_uv, w_o, ln1_g, ln1_b, w_up, conv_w, conv_b, w_down, ln2_g, ln2_b)
    scale = math.log2(math.e) / math.sqrt(MLA_NOPE_DIM + MLA_ROPE_DIM)
    tables = _rope_tables(S, _freq_lanes(), scale)
    q, k, vt, qkv_dil = _projection(x, p["w_in_p"], p["gq"], p["gkv"], p["wq2"], p["wk"], p["wvt"], tables,
                                    p["vone"])
    o_mla = _mla_attention(q, k, vt)
    o_dils, lses = _dilated_attention(qkv_dil)
    x1 = _out_projection(o_mla, o_dils, lses, x, p["w_o"], p["expand"], *p["ln1"])
    return _ffn(x1, p["w_up"], p["cw"], p["wd"], *p["ln2"])
```

```python
import functools
import math

import jax
import jax.numpy as jnp
from jax import lax
from jax.experimental import pallas as pl
from jax.experimental.pallas import tpu as pltpu

D_MODEL = 1024
HEAD_DIM = 64
MLA_HEADS = 8
MLA_Q_RANK = 256
MLA_KV_RANK = 128
MLA_NOPE_DIM = 64
MLA_ROPE_DIM = 32
ROPE_THETA = 10000.0
DIL_HEADS = 8
DIL_PAIRS = ((128, 1), (512, 4), (2048, 16))
DIL_BLOCK = 128
DIL_WIDTH = DIL_HEADS * HEAD_DIM
D_FF = 2816
CONV_WIDTH = 3
DEPTH = 1
DN_ALPHA = (2.0 * DEPTH) ** 0.25
LN_EPS = 1e-5
RMS_EPS = 1e-6

LANES = 128
BF16_ROWS = 16
F32_ROWS = 8
VMEM_LIMIT = 56 * 1024 * 1024

PROJ_ROWS = 512
OUT_ROWS = 1024
DIL_SPAN = 2048
FFN_ROWS = 1024
FFN_CHUNK = 256
NEG_BIG = 1e30

BF16 = jnp.bfloat16
F32 = jnp.float32


def _dot(a, b):
    return jnp.dot(a, b, preferred_element_type=F32)


def _dot_nt(a, b):
    return lax.dot_general(a, b, (((1,), (1,)), ((), ())), preferred_element_type=F32)


def _params(*sem, flags=None):
    return pltpu.CompilerParams(dimension_semantics=sem, vmem_limit_bytes=VMEM_LIMIT, flags=flags)


def _const_spec(shape):
    zeros = (0,) * len(shape)
    return pl.BlockSpec(shape, lambda *_: zeros, pipeline_mode=pl.Buffered(1))


def _rope_table_kernel(freq_ref, out_ref, *, scale):
    rows = out_ref.shape[1]
    blocks = rows // LANES
    coarse_rows = -(-blocks // F32_ROWS) * F32_ROWS
    fine = lax.broadcasted_iota(jnp.int32, (LANES, LANES), 0).astype(F32) * freq_ref[...]
    coarse = (lax.broadcasted_iota(jnp.int32, (coarse_rows, LANES), 0) * LANES).astype(F32) * freq_ref[...]
    cos_b, sin_b = jnp.cos(fine), jnp.sin(fine)
    cos_a, sin_a = jnp.cos(coarse), jnp.sin(coarse)
    for a in range(blocks):
        ca, sa = cos_a[a:a + 1, :], sin_a[a:a + 1, :]
        c = ca * cos_b - sa * sin_b
        s = sa * cos_b + ca * sin_b
        band = slice(a * LANES, (a + 1) * LANES)
        out_ref[0, band, :] = c
        out_ref[1, band, :] = s
        out_ref[2, band, :] = c * scale
        out_ref[3, band, :] = s * scale


def _rope_tables(seq, freq_lanes, scale):
    return pl.pallas_call(
        functools.partial(_rope_table_kernel, scale=scale),
        out_shape=jax.ShapeDtypeStruct((4, seq, LANES), F32),
        name="rope_tables",
    )(freq_lanes)


def _rms(x, g):
    ms = jnp.mean(x * x, axis=-1, keepdims=True)
    return x * lax.rsqrt(ms + RMS_EPS) * g


def _proj_kernel(x_ref, w_in_ref, gq_ref, gkv_ref, wq_ref, wk_ref, wvt_ref, tab_ref, vone_ref,
                 q_ref, k_ref, vt_ref, *rest):
    dil_refs, stage_refs = rest[:-2], rest[-2:]
    rows = x_ref.shape[1]
    xb = x_ref[0].astype(BF16)
    h = _dot(xb, w_in_ref[...])
    cq = _rms(h[:, :MLA_Q_RANK], gq_ref[...]).astype(BF16)
    ckv = _rms(h[:, MLA_Q_RANK:MLA_Q_RANK + MLA_KV_RANK], gkv_ref[...]).astype(BF16)
    kr = h[:, 3 * LANES:4 * LANES]
    cos, sin, cos_q, sin_q = tab_ref[0], tab_ref[1], tab_ref[2], tab_ref[3]

    k_plain = pltpu.roll(kr, 64, 1)
    k_swap = pltpu.roll(kr, 80, 1) - pltpu.roll(kr, 48, 1)
    k_rope = k_plain * cos + k_swap * sin

    q2 = _dot(cq, wq_ref[...])
    kn = _dot(ckv, wk_ref[...])
    vt = _dot_nt(wvt_ref[...].reshape(MLA_HEADS * LANES, MLA_KV_RANK), ckv)
    hw = MLA_HEADS * LANES
    for hd in range(MLA_HEADS):
        lo = hd * LANES
        qh = q2[:, lo:lo + LANES] * cos_q + q2[:, hw + lo:hw + lo + LANES] * sin_q
        q_ref[0, hd] = qh.astype(BF16)
        k_ref[0, hd] = (kn[:, lo:lo + LANES] + k_rope).astype(BF16)
        vt_ref[0, hd, 0] = (vt[lo:lo + LANES, :] + vone_ref[...]).astype(BF16)

    base = 4 * LANES
    groups = DIL_WIDTH // LANES
    assert [d for _, d in DIL_PAIRS] == [1, 4, 16]
    flat_ref, by4_ref = stage_refs
    for c in range(3 * groups):
        slab = h[:, base + c * LANES:base + (c + 1) * LANES]
        slab = slab * (math.log2(math.e) / math.sqrt(HEAD_DIM)) if c < groups else slab
        lanes = slice((c % groups) * LANES, (c % groups + 1) * LANES)
        flat_ref[c] = slab
        dil_refs[c // groups][0, 0, :, lanes] = slab.astype(BF16)
        for r4 in range(4):
            piece = flat_ref[c, pl.ds(r4, rows // 4, stride=4), :]
            by4_ref[c, r4] = piece
            dil_refs[3 + c // groups][0, r4, :, lanes] = piece.astype(BF16)
            for r in range(4):
                piece16 = by4_ref[c, r4, pl.ds(r, rows // 16, stride=4), :]
                dil_refs[6 + c // groups][0, r4 + 4 * r, :, lanes] = piece16.astype(BF16)


def _projection(x, w_in_p, gq, gkv, wq2, wk, wvt, tables, vone):
    B, S, _ = x.shape
    rows = min(PROJ_ROWS, S)
    head_shape = jax.ShapeDtypeStruct((B, MLA_HEADS, S, LANES), BF16)
    vt_shape = jax.ShapeDtypeStruct((B, MLA_HEADS, S // rows, LANES, rows), BF16)
    head_spec = pl.BlockSpec((1, MLA_HEADS, rows, LANES), lambda b, i: (b, 0, i, 0))
    vt_spec = pl.BlockSpec((1, MLA_HEADS, 1, LANES, rows), lambda b, i: (b, 0, i, 0, 0))
    dil_shapes, dil_specs = [], []
    for _, dil in DIL_PAIRS:
        assert rows % (dil * BF16_ROWS) == 0
        dil_shapes += [jax.ShapeDtypeStruct((B, dil, S // dil, DIL_WIDTH), BF16)] * 3
        dil_specs += [pl.BlockSpec((1, dil, rows // dil, DIL_WIDTH), lambda b, i: (b, 0, i, 0))] * 3
    slabs = 3 * DIL_WIDTH // LANES
    outs = pl.pallas_call(
        _proj_kernel,
        grid=(B, S // rows),
        in_specs=[
            pl.BlockSpec((1, rows, D_MODEL), lambda b, i: (b, i, 0)),
            _const_spec(w_in_p.shape),
            _const_spec(gq.shape),
            _const_spec(gkv.shape),
            _const_spec(wq2.shape),
            _const_spec(wk.shape),
            _const_spec(wvt.shape),
            pl.BlockSpec((4, rows, LANES), lambda b, i: (0, i, 0)),
            _const_spec(vone.shape),
        ],
        out_specs=[head_spec, head_spec, vt_spec] + dil_specs,
        out_shape=[head_shape, head_shape, vt_shape] + dil_shapes,
        scratch_shapes=[pltpu.VMEM((slabs, rows, LANES), F32),
                        pltpu.VMEM((slabs, 4, rows // 4, LANES), F32)],
        compiler_params=_params("parallel", "parallel"),
        name="in_projection",
    )(x, w_in_p, gq, gkv, wq2, wk, wvt, tables, vone)
    return outs[0], outs[1], outs[2], [outs[3 + 3 * d:6 + 3 * d] for d in range(len(DIL_PAIRS))]


def _mla_kernel(q_ref, k_ref, vt_ref, o_ref, acc_ref, m_ref, st_ref, pt_ref, *, blk):
    nq = q_ref.shape[2] // blk
    strip = 64
    unroll = 8
    m_ref[...] = jnp.full_like(m_ref, -NEG_BIG)
    acc_ref[...] = jnp.zeros_like(acc_ref)

    def rows(i):
        return pl.ds(i * blk if isinstance(i, int) else pl.multiple_of(i * blk, blk), blk)

    half = blk // 2
    assert half % strip == 0

    def half_rows(i, which):
        start = i * blk + which * half
        return pl.ds(start if isinstance(i, int) else pl.multiple_of(start, half), half)

    def scores(item, slot, masked):
        qi, j = item
        for hh in range(2):
            if not masked:
                st_ref[hh, slot] = _dot_nt(k_ref[0, hh, rows(j), :], q_ref[0, hh, rows(qi), :])
            else:
                st_ref[hh, slot, :half, :] = _dot_nt(k_ref[0, hh, half_rows(j, 0), :], q_ref[0, hh, rows(qi), :])
                st_ref[hh, slot, half:, half:] = _dot_nt(k_ref[0, hh, half_rows(j, 1), :],
                                                         q_ref[0, hh, half_rows(qi, 1), :])

    def update(item, slot, masked):
        qi, j = item

        def band(hh, s):
            if not masked:
                return st_ref[hh, slot, s * strip:(s + 1) * strip, :]
            lo = 0 if s * strip < half else half
            tile = st_ref[hh, slot, s * strip:(s + 1) * strip, lo:]
            key = lax.broadcasted_iota(jnp.int32, tile.shape, 0) + s * strip
            qry = lax.broadcasted_iota(jnp.int32, tile.shape, 1) + lo
            return jnp.where(key <= qry, tile, -NEG_BIG)

        def fold(parts):
            top = parts[0]
            for part in parts[1:]:
                top = jnp.maximum(top, part)
            return top

        for hh in range(2):
            tops = [band(hh, s).reshape(strip // 8, 8, -1).max(axis=0) for s in range(blk // strip)]
            if masked:
                n_lo = half // strip
                wide, narrow = fold(tops[:n_lo]), fold(tops[n_lo:])
                top = jnp.concatenate([wide[:, :half], jnp.maximum(wide[:, half:], narrow)], axis=1)
            else:
                top = fold(tops)
            m_old = m_ref[qi, hh]
            m_new = jnp.maximum(m_old, jnp.max(top, axis=0, keepdims=True))
            for s in range(blk // strip):
                tile = band(hh, s)
                lo = blk - tile.shape[1]
                pt_ref[hh, s * strip:(s + 1) * strip, lo:] = jnp.exp2(tile - m_new[:, lo:]).astype(BF16)
            if masked:
                vt = vt_ref[0, hh, j]
                pv = _dot(vt[:, :half], pt_ref[hh, :half, :])
                pv_hi = _dot(vt[:, half:], pt_ref[hh, half:, half:])
                pv = jnp.concatenate([pv[:, :half], pv[:, half:] + pv_hi], axis=1)
            else:
                pv = _dot(vt_ref[0, hh, j], pt_ref[hh])
            acc_ref[qi, hh] = acc_ref[qi, hh] * jnp.exp2(m_old - m_new) + pv
            m_ref[qi, hh] = m_new

    def sweep(items, advance, masked):
        n = len(items)
        peel = (n - 1) % unroll if n - 1 > unroll else n - 1
        scores(items[0], 0, masked)
        for t in range(peel):
            scores(items[t + 1], (t + 1) % 2, masked)
            update(items[t], t % 2, masked)

        def body(_, cur):
            for k in range(unroll):
                nxt = advance(cur)
                scores(nxt, (peel + k + 1) % 2, masked)
                update(cur, (peel + k) % 2, masked)
                cur = nxt
            return cur

        if peel < n - 1:
            start = (jnp.int32(items[peel][0]), jnp.int32(items[peel][1]))
            lax.fori_loop(0, (n - 1 - peel) // unroll, body, start)
        update(items[n - 1], (n - 1) % 2, masked)

    def next_below(item):
        qi, j = item
        wrap = j + 1 >= qi
        return jnp.where(wrap, qi + 1, qi), jnp.where(wrap, 0, j + 1)

    sweep([(qi, j) for qi in range(1, nq) for j in range(qi)], next_below, False)
    sweep([(qi, qi) for qi in range(nq)], lambda item: (item[0] + 1, item[1] + 1), True)
    for qi in range(nq):
        outs = []
        for hh in range(2):
            acc = acc_ref[qi, hh]
            outs.append(acc[:HEAD_DIM] / acc[HEAD_DIM:HEAD_DIM + 1])
        o_ref[0, rows(qi), :] = jnp.concatenate(outs, axis=0).T.astype(o_ref.dtype)


def _mla_attention(q, k, vt):
    B, H, S, _ = q.shape
    blk = vt.shape[-1]
    qk_spec = pl.BlockSpec((1, 2, S, LANES), lambda b, hp: (b, hp, 0, 0))
    vt_spec = pl.BlockSpec((1, 2, S // blk, LANES, blk), lambda b, hp: (b, hp, 0, 0, 0))
    return pl.pallas_call(
        functools.partial(_mla_kernel, blk=blk),
        grid=(B, H // 2),
        in_specs=[qk_spec, qk_spec, vt_spec],
        out_specs=pl.BlockSpec((1, S, LANES), lambda b, hp: (b, 0, hp)),
        out_shape=jax.ShapeDtypeStruct((B, S, H * HEAD_DIM), BF16),
        scratch_shapes=[pltpu.VMEM((S // blk, 2, LANES, blk), F32),
                        pltpu.VMEM((S // blk, 2, 1, blk), F32),
                        pltpu.VMEM((2, 2, blk, blk), F32),
                        pltpu.VMEM((2, blk, blk), BF16)],
        compiler_params=_params("parallel", "parallel"),
        name="mla_attention",
    )(q, k, vt)


def _dil_bias_tiles(bias_ref, dil):
    blk = DIL_BLOCK
    key = lax.broadcasted_iota(jnp.int32, (2 * blk, blk), 0)
    qry = lax.broadcasted_iota(jnp.int32, (2 * blk, blk), 1)
    off = qry + blk - key
    steps = jnp.where((off >= 0) & (off <= blk), off.astype(F32), NEG_BIG)
    for pair in range(DIL_HEADS // 2):
        halves = []
        for hh in range(2):
            slope = 2.0 ** (-8.0 * (2 * pair + hh + 1) / DIL_HEADS)
            halves.append(steps * (-slope * dil * math.log2(math.e)))
        tile = jnp.concatenate(halves, axis=1)
        bias_ref[0, pair] = tile
        bias_ref[1, pair, :blk, :] = jnp.full((blk, 2 * blk), -NEG_BIG, F32)
        bias_ref[1, pair, blk:, :] = tile[blk:]


def _dil_kernel(q_ref, kc_ref, kp_ref, vc_ref, vp_ref, o_ref, lse_ref, st_scr, o_scr, lse_scr, bias_ref, *,
                dil, lane0):
    blk = DIL_BLOCK
    pairs = DIL_HEADS // 2
    nb = q_ref.shape[2] // blk
    units = dil * nb
    pl.when((pl.program_id(0) == 0) & (pl.program_id(1) == 0))(functools.partial(_dil_bias_tiles, bias_ref, dil))
    first_span = pl.program_id(1) == 0
    lane = lax.broadcasted_iota(jnp.int32, (blk, LANES), 1)
    first = lane < HEAD_DIM

    def split(u):
        return u // nb, u % nb

    def row0(i):
        return i * blk

    def window(which, r, i, cols):
        cur_ref, prev_ref = ((kc_ref, kp_ref), (vc_ref, vp_ref))[which]
        if i == 0:
            return jnp.concatenate([prev_ref[0, r, :, cols], cur_ref[0, r, 0:blk, cols]], axis=0)
        return cur_ref[0, r, pl.ds(row0(i - 1), 2 * blk), cols]

    def scores(u, slot):
        r, i = split(u)
        q = q_ref[0, r, pl.ds(row0(i), blk), :]
        no_prev = first_span.astype(jnp.int32) if i == 0 else 0
        for pair in range(pairs):
            cols = slice(pair * LANES, (pair + 1) * LANES)
            qp = q[:, cols]
            zero = jnp.zeros_like(qp)
            q2 = jnp.concatenate([jnp.where(first, qp, zero), jnp.where(first, zero, qp)], axis=0)
            st_scr[slot, pair] = _dot_nt(window(0, r, i, cols), q2) + bias_ref[no_prev, pair]

    def finish(u, slot):
        r, i = split(u)
        if dil == 1:
            rows = pl.ds(row0(i), blk)
        else:
            rows = pl.ds(r + i * (blk * dil), blk, stride=dil)
        lse_rows = []
        for pair in range(pairs):
            cols = slice(pair * LANES, (pair + 1) * LANES)
            st = st_scr[slot, pair]
            m = jnp.max(st, axis=0, keepdims=True)
            pt = jnp.exp2(st - m).astype(BF16)
            vt = jnp.concatenate([window(1, r, i, cols).T, jnp.ones((BF16_ROWS, 2 * blk), BF16)], axis=0)
            res = _dot(vt, pt)
            l = res[LANES:LANES + 1, :]
            inv = 1.0 / l
            ot = jnp.concatenate([res[:HEAD_DIM, :LANES] * inv[:, :LANES],
                                  res[HEAD_DIM:LANES, LANES:] * inv[:, LANES:]], axis=0)
            o_scr[pair, rows, :] = ot.T
            lse2 = m + jnp.log2(l)
            lse_rows += [lse2[:, :LANES], lse2[:, LANES:]]
        tile = jnp.concatenate([jnp.zeros((lane0, blk), F32)] * (lane0 > 0) + lse_rows
                               + [jnp.zeros((LANES - lane0 - DIL_HEADS, blk), F32)], axis=0)
        lse_scr[rows, :] = tile.T

    scores(0, 0)
    for u in range(units):
        if u + 1 < units:
            scores(u + 1, (u + 1) % 2)
        finish(u, u % 2)
    for pair in range(pairs):
        o_ref[0, :, pair * LANES:(pair + 1) * LANES] = o_scr[pair].astype(o_ref.dtype)
    lse_ref[0] = lse_scr[...]


def _dil_branch(q, k, v, dil, lane0):
    B, _, M, W = q.shape
    S = M * dil
    span = min(DIL_SPAN, S)
    assert span % (dil * DIL_BLOCK) == 0 and S % span == 0
    rows = span // dil
    nb = rows // DIL_BLOCK
    cur = pl.BlockSpec((1, dil, rows, W), lambda b, c: (b, 0, c, 0))
    prev = pl.BlockSpec((1, dil, DIL_BLOCK, W), lambda b, c: (b, 0, jnp.maximum(c * nb - 1, 0), 0))
    return pl.pallas_call(
        functools.partial(_dil_kernel, dil=dil, lane0=lane0),
        grid=(B, S // span),
        in_specs=[cur, cur, prev, cur, prev],
        out_specs=[pl.BlockSpec((1, span, W), lambda b, c: (b, c, 0)),
                   pl.BlockSpec((1, span, LANES), lambda b, c: (b, c, 0))],
        out_shape=[jax.ShapeDtypeStruct((B, S, W), BF16), jax.ShapeDtypeStruct((B, S, LANES), F32)],
        scratch_shapes=[pltpu.VMEM((2, DIL_HEADS // 2, 2 * DIL_BLOCK, 2 * LANES), F32),
                        pltpu.VMEM((DIL_HEADS // 2, span, LANES), F32),
                        pltpu.VMEM((span, LANES), F32),
                        pltpu.VMEM((2, DIL_HEADS // 2, 2 * DIL_BLOCK, 2 * LANES), F32)],
        compiler_params=_params("arbitrary", "arbitrary"),
        name=f"dilated_d{dil}",
    )(q, k, k, v, v)


def _dilated_attention(qkv_per_dil):
    outs, lses = [], []
    for i, ((window, dil), (q, k, v)) in enumerate(zip(DIL_PAIRS, qkv_per_dil)):
        assert window // dil == DIL_BLOCK
        o, lse = _dil_branch(q, k, v, dil, i * DIL_HEADS)
        outs.append(o)
        lses.append(lse)
    return outs, lses


def _layer_norm(y, g, b):
    mu = jnp.mean(y, axis=-1, keepdims=True)
    d = y - mu
    var = jnp.mean(d * d, axis=-1, keepdims=True)
    return d * lax.rsqrt(var + LN_EPS) * g + b


def _out_proj_kernel(om_ref, o0_ref, o1_ref, o2_ref, lse0_ref, lse1_ref, lse2_ref, x_ref, wo_ref, ex_ref,
                     g_ref, b_ref, o_ref):
    H, W = DIL_HEADS, DIL_WIDTH
    l0 = lse0_ref[0] + lse1_ref[0] + lse2_ref[0]
    l1 = pltpu.roll(l0, LANES - H, 1)
    l2 = pltpu.roll(l0, LANES - 2 * H, 1)
    top = jnp.maximum(l0, jnp.maximum(l1, l2))
    e0, e1, e2 = jnp.exp2(l0 - top), jnp.exp2(l1 - top), jnp.exp2(l2 - top)
    inv = 1.0 / (e0 + e1 + e2)
    lane = lax.broadcasted_iota(jnp.int32, l0.shape, 1)
    w = jnp.where(lane < H, e0 * inv,
                  jnp.where(lane < 2 * H, pltpu.roll(e1 * inv, H, 1), pltpu.roll(e2 * inv, 2 * H, 1)))
    wide = _dot(w.astype(BF16), ex_ref[...])
    o_dil = (wide[:, :W] * o0_ref[0] + wide[:, W:2 * W] * o1_ref[0] + wide[:, 2 * W:] * o2_ref[0]).astype(BF16)
    half = om_ref.shape[-1]
    mix = _dot(om_ref[0], wo_ref[:half, :]) + _dot(o_dil, wo_ref[half:, :])
    o_ref[0] = _layer_norm(DN_ALPHA * x_ref[0] + mix, g_ref[...], b_ref[...])


def _out_projection(o_mla, o_dils, lses, x, w_o, expand, g, b):
    B, S, _ = x.shape
    rows = min(OUT_ROWS, S)
    half_spec = pl.BlockSpec((1, rows, o_mla.shape[-1]), lambda bi, i: (bi, i, 0))
    lse_spec = pl.BlockSpec((1, rows, LANES), lambda bi, i: (bi, i, 0))
    x_spec = pl.BlockSpec((1, rows, D_MODEL), lambda bi, i: (bi, i, 0))
    return pl.pallas_call(
        _out_proj_kernel,
        grid=(B, S // rows),
        in_specs=[half_spec, half_spec, half_spec, half_spec, lse_spec, lse_spec, lse_spec, x_spec,
                  _const_spec(w_o.shape), _const_spec(expand.shape), _const_spec(g.shape), _const_spec(b.shape)],
        out_specs=x_spec,
        out_shape=jax.ShapeDtypeStruct(x.shape, F32),
        compiler_params=_params("parallel", "parallel"),
        name="out_projection_ln",
    )(o_mla, *o_dils, *lses, x, w_o, expand, g, b)


def _ffn_kernel(x_ref, wup_ref, cw_ref, wd_ref, g_ref, b_ref, o_ref, acc_ref, u0_ref, u1_ref, xe_ref, tail_ref):
    rows = x_ref.shape[1]
    pad = tail_ref.shape[2]
    chunks = D_FF // FFN_CHUNK

    def cols(c, part=0):
        start = c * FFN_CHUNK + part * D_FF
        return pl.ds(start if isinstance(c, int) else pl.multiple_of(start, FFN_CHUNK), FFN_CHUNK)

    @pl.when(pl.program_id(1) == 0)
    def _():
        tail_ref[...] = jnp.zeros_like(tail_ref)

    xe_ref[...] = x_ref[0].astype(BF16)
    acc_ref[...] = jnp.zeros_like(acc_ref)

    def up(c, u_ref):
        for part in range(2):
            u = _dot(xe_ref[...], wup_ref[:, cols(c, part)])
            u_ref[part, :pad, :] = tail_ref[c, part]
            u_ref[part, pad:, :] = u
            tail_ref[c, part] = u[rows - pad:, :]

    def conv(u_ref, part, taps):
        y = taps[3:4, :]
        for j in range(CONV_WIDTH):
            shift = CONV_WIDTH - 1 - j
            y = y + taps[j:j + 1, :] * u_ref[part, pl.ds(pad - shift, rows), :]
        return y

    def down(c, u_ref):
        ya = conv(u_ref, 0, cw_ref[:, cols(c, 0)])
        yg = conv(u_ref, 1, cw_ref[:, cols(c, 1)])
        c1 = math.sqrt(2.0 / math.pi)
        th = jnp.tanh(yg * (c1 + (c1 * 0.044715) * (yg * yg)))
        hidden = ((yg + yg * th) * ya).astype(BF16)
        acc_ref[...] += _dot(hidden, wd_ref[cols(c), :])

    assert chunks % 2 == 1
    up(0, u0_ref)

    def body(j, carry):
        c = 2 * j
        up(c + 1, u1_ref)
        down(c, u0_ref)
        up(c + 2, u0_ref)
        down(c + 1, u1_ref)
        return carry

    lax.fori_loop(0, chunks // 2, body, 0)
    down(chunks - 1, u0_ref)
    o_ref[0] = _layer_norm(DN_ALPHA * x_ref[0] + acc_ref[...], g_ref[...], b_ref[...])


def _ffn(x1, w_up, cw, wd, g, b):
    B, S, _ = x1.shape
    rows = min(FFN_ROWS, S)
    pad = F32_ROWS
    x_spec = pl.BlockSpec((1, rows, D_MODEL), lambda bi, i: (bi, i, 0))
    return pl.pallas_call(
        _ffn_kernel,
        grid=(B, S // rows),
        in_specs=[x_spec, _const_spec(w_up.shape), _const_spec(cw.shape),
                  _const_spec(wd.shape), _const_spec(g.shape), _const_spec(b.shape)],
        out_specs=x_spec,
        out_shape=jax.ShapeDtypeStruct(x1.shape, F32),
        scratch_shapes=[pltpu.VMEM((rows, D_MODEL), F32),
                        pltpu.VMEM((2, pad + rows, FFN_CHUNK), F32), pltpu.VMEM((2, pad + rows, FFN_CHUNK), F32),
                        pltpu.VMEM((rows, D_MODEL), BF16),
                        pltpu.VMEM((D_FF // FFN_CHUNK, 2, pad, FFN_CHUNK), F32)],
        compiler_params=_params("parallel", "arbitrary"),
        name="conv_ffn_ln",
    )(x1, w_up, cw, wd, g, b)


def _pad_cols(w, width):
    return jnp.pad(w, ((0, 0), (0, width - w.shape[1])))


def _head_groups(w):
    rank, heads, e = w.shape
    return jnp.pad(w, ((0, 0), (0, 0), (0, LANES - e))).reshape(rank, heads * LANES)


def _prepare(w_in, g_cq, g_ckv, w_uq, w_uk, w_uv, w_o, ln1_g, ln1_b, w_up, conv_w, conv_b, w_down, ln2_g, ln2_b):
    r0, r1, r2 = MLA_Q_RANK, MLA_Q_RANK + MLA_KV_RANK, MLA_Q_RANK + MLA_KV_RANK + MLA_ROPE_DIM
    w_in_p = jnp.concatenate([w_in[:, :r1], _pad_cols(w_in[:, r1:r2], LANES), w_in[:, r2:]], axis=1).astype(BF16)

    half = MLA_ROPE_DIM // 2
    rope = w_uq[:, :, MLA_NOPE_DIM:]
    swapped = jnp.concatenate([-rope[:, :, half:], rope[:, :, :half]], axis=-1)
    swapped = jnp.concatenate([jnp.zeros_like(w_uq[:, :, :MLA_NOPE_DIM]), swapped], axis=-1)
    wq2 = jnp.concatenate([_head_groups(w_uq), _head_groups(swapped)], axis=1).astype(BF16)

    wk = _head_groups(w_uk).astype(BF16)
    wvt = jnp.pad(w_uv.transpose(1, 2, 0), ((0, 0), (0, LANES - HEAD_DIM), (0, 0))).astype(BF16)
    vone = jnp.zeros((LANES, 1), F32).at[HEAD_DIM, 0].set(1.0)

    assert D_FF % FFN_CHUNK == 0
    taps = jnp.concatenate([conv_w, conv_b[None, :]], axis=0)
    cw = jnp.concatenate([0.5 * taps[:, :D_FF], taps[:, D_FF:]], axis=1)
    src = jnp.arange(LANES)[:, None]
    dst = jnp.arange(len(DIL_PAIRS) * DIL_WIDTH)[None, :]
    expand = (src == dst // HEAD_DIM).astype(BF16)
    row = lambda a: a.reshape(1, -1)
    return dict(w_in_p=w_in_p, gq=row(g_cq), gkv=row(g_ckv), wq2=wq2, wk=wk, wvt=wvt, vone=vone,
                w_o=w_o.astype(BF16), expand=expand, ln1=(row(ln1_g), row(ln1_b)), w_up=w_up.astype(BF16), cw=cw,
                wd=w_down.astype(BF16),
                ln2=(row(ln2_g), row(ln2_b)))


def _freq_lanes():
    half = MLA_ROPE_DIM // 2
    freqs = ROPE_THETA ** (-jnp.arange(half, dtype=F32) / half)
    zeros = jnp.zeros((MLA_NOPE_DIM,), F32)
    return jnp.concatenate([zeros, freqs, freqs, jnp.zeros((LANES - MLA_NOPE_DIM - MLA_ROPE_DIM,), F32)])[None, :]


def kernel(x, w_in, g_cq, g_ckv, w_uq, w_uk, w_uv, w_o, ln1_g, ln1_b, w_up, conv_w, conv_b, w_down, ln2_g, ln2_b):
    B, S, _ = x.shape
    p = _prepare(w_in, g_cq, g_ckv, w_uq, w_uk, w_uv, w_o, ln1_g, ln1_b, w_up, conv_w, conv_b, w_down, ln2_g, ln2_b)
    scale = math.log2(math.e) / math.sqrt(MLA_NOPE_DIM + MLA_ROPE_DIM)
    tables = _rope_tables(S, _freq_lanes(), scale)
    q, k, vt, qkv_dil = _projection(x, p["w_in_p"], p["gq"], p["gkv"], p["wq2"], p["wk"], p["wvt"], tables,
                                    p["vone"])
    o_mla = _mla_attention(q, k, vt)
    o_dils, lses = _dilated_attention(qkv_dil)
    x1 = _out_projection(o_mla, o_dils, lses, x, p["w_o"], p["expand"], *p["ln1"])
    return _ffn(x1, p["w_up"], p["cw"], p["wd"], *p["ln2"])
```

```python
import functools
import math

import jax
import jax.numpy as jnp
from jax import lax
from jax.experimental import pallas as pl
from jax.experimental.pallas import tpu as pltpu

D_MODEL = 1024
HEAD_DIM = 64
MLA_HEADS = 8
MLA_Q_RANK = 256
MLA_KV_RANK = 128
MLA_NOPE_DIM = 64
MLA_ROPE_DIM = 32
ROPE_THETA = 10000.0
DIL_HEADS = 8
DIL_PAIRS = ((128, 1), (512, 4), (2048, 16))
DIL_BLOCK = 128
DIL_WIDTH = DIL_HEADS * HEAD_DIM
D_FF = 2816
CONV_WIDTH = 3
DEPTH = 1
DN_ALPHA = (2.0 * DEPTH) ** 0.25
LN_EPS = 1e-5
RMS_EPS = 1e-6

LANES = 128
BF16_ROWS = 16
F32_ROWS = 8
VMEM_LIMIT = 56 * 1024 * 1024

PROJ_ROWS = 512
OUT_ROWS = 1024
DIL_SPAN = 2048
FFN_ROWS = 1024
FFN_CHUNK = 256
NEG_BIG = 1e30

BF16 = jnp.bfloat16
F32 = jnp.float32


def _dot(a, b):
    return jnp.dot(a, b, preferred_element_type=F32)


def _dot_nt(a, b):
    return lax.dot_general(a, b, (((1,), (1,)), ((), ())), preferred_element_type=F32)


def _params(*sem, flags=None):
    return pltpu.CompilerParams(dimension_semantics=sem, vmem_limit_bytes=VMEM_LIMIT, flags=flags)


def _const_spec(shape):
    zeros = (0,) * len(shape)
    return pl.BlockSpec(shape, lambda *_: zeros, pipeline_mode=pl.Buffered(1))


def _rope_table_kernel(freq_ref, out_ref, *, scale):
    rows = out_ref.shape[1]
    blocks = rows // LANES
    coarse_rows = -(-blocks // F32_ROWS) * F32_ROWS
    fine = lax.broadcasted_iota(jnp.int32, (LANES, LANES), 0).astype(F32) * freq_ref[...]
    coarse = (lax.broadcasted_iota(jnp.int32, (coarse_rows, LANES), 0) * LANES).astype(F32) * freq_ref[...]
    cos_b, sin_b = jnp.cos(fine), jnp.sin(fine)
    cos_a, sin_a = jnp.cos(coarse), jnp.sin(coarse)
    for a in range(blocks):
        ca, sa = cos_a[a:a + 1, :], sin_a[a:a + 1, :]
        c = ca * cos_b - sa * sin_b
        s = sa * cos_b + ca * sin_b
        band = slice(a * LANES, (a + 1) * LANES)
        out_ref[0, band, :] = c
        out_ref[1, band, :] = s
        out_ref[2, band, :] = c * scale
        out_ref[3, band, :] = s * scale


def _rope_tables(seq, freq_lanes, scale):
    return pl.pallas_call(
        functools.partial(_rope_table_kernel, scale=scale),
        out_shape=jax.ShapeDtypeStruct((4, seq, LANES), F32),
        name="rope_tables",
    )(freq_lanes)


def _rms(x, g):
    ms = jnp.mean(x * x, axis=-1, keepdims=True)
    return x * lax.rsqrt(ms + RMS_EPS) * g


def _proj_kernel(x_ref, w_in_ref, gq_ref, gkv_ref, wq_ref, wk_ref, wvt_ref, tab_ref, vone_ref,
                 q_ref, k_ref, vt_ref, *rest):
    dil_refs, stage_refs = rest[:-2], rest[-2:]
    rows = x_ref.shape[1]
    xb = x_ref[0].astype(BF16)
    h = _dot(xb, w_in_ref[...])
    cq = _rms(h[:, :MLA_Q_RANK], gq_ref[...]).astype(BF16)
    ckv = _rms(h[:, MLA_Q_RANK:MLA_Q_RANK + MLA_KV_RANK], gkv_ref[...]).astype(BF16)
    kr = h[:, 3 * LANES:4 * LANES]
    cos, sin, cos_q, sin_q = tab_ref[0], tab_ref[1], tab_ref[2], tab_ref[3]

    k_plain = pltpu.roll(kr, 64, 1)
    k_swap = pltpu.roll(kr, 80, 1) - pltpu.roll(kr, 48, 1)
    k_rope = k_plain * cos + k_swap * sin

    q2 = _dot(cq, wq_ref[...])
    kn = _dot(ckv, wk_ref[...])
    vt = _dot_nt(wvt_ref[...].reshape(MLA_HEADS * LANES, MLA_KV_RANK), ckv)
    hw = MLA_HEADS * LANES
    for hd in range(MLA_HEADS):
        lo = hd * LANES
        qh = q2[:, lo:lo + LANES] * cos_q + q2[:, hw + lo:hw + lo + LANES] * sin_q
        q_ref[0, hd] = qh.astype(BF16)
        k_ref[0, hd] = (kn[:, lo:lo + LANES] + k_rope).astype(BF16)
        vt_ref[0, hd, 0] = (vt[lo:lo + LANES, :] + vone_ref[...]).astype(BF16)

    base = 4 * LANES
    groups = DIL_WIDTH // LANES
    assert [d for _, d in DIL_PAIRS] == [1, 4, 16]
    flat_ref, by4_ref = stage_refs
    for c in range(3 * groups):
        slab = h[:, base + c * LANES:base + (c + 1) * LANES]
        slab = slab * (math.log2(math.e) / math.sqrt(HEAD_DIM)) if c < groups else slab
        lanes = slice((c % groups) * LANES, (c % groups + 1) * LANES)
        flat_ref[c] = slab
        dil_refs[c // groups][0, 0, :, lanes] = slab.astype(BF16)
        for r4 in range(4):
            piece = flat_ref[c, pl.ds(r4, rows // 4, stride=4), :]
            by4_ref[c, r4] = piece
            dil_refs[3 + c // groups][0, r4, :, lanes] = piece.astype(BF16)
            for r in range(4):
                piece16 = by4_ref[c, r4, pl.ds(r, rows // 16, stride=4), :]
                dil_refs[6 + c // groups][0, r4 + 4 * r, :, lanes] = piece16.astype(BF16)


def _projection(x, w_in_p, gq, gkv, wq2, wk, wvt, tables, vone):
    B, S, _ = x.shape
    rows = min(PROJ_ROWS, S)
    head_shape = jax.ShapeDtypeStruct((B, MLA_HEADS, S, LANES), BF16)
    vt_shape = jax.ShapeDtypeStruct((B, MLA_HEADS, S // rows, LANES, rows), BF16)
    head_spec = pl.BlockSpec((1, MLA_HEADS, rows, LANES), lambda b, i: (b, 0, i, 0))
    vt_spec = pl.BlockSpec((1, MLA_HEADS, 1, LANES, rows), lambda b, i: (b, 0, i, 0, 0))
    dil_shapes, dil_specs = [], []
    for _, dil in DIL_PAIRS:
        assert rows % (dil * BF16_ROWS) == 0
        dil_shapes += [jax.ShapeDtypeStruct((B, dil, S // dil, DIL_WIDTH), BF16)] * 3
        dil_specs += [pl.BlockSpec((1, dil, rows // dil, DIL_WIDTH), lambda b, i: (b, 0, i, 0))] * 3
    slabs = 3 * DIL_WIDTH // LANES
    outs = pl.pallas_call(
        _proj_kernel,
        grid=(B, S // rows),
        in_specs=[
            pl.BlockSpec((1, rows, D_MODEL), lambda b, i: (b, i, 0)),
            _const_spec(w_in_p.shape),
            _const_spec(gq.shape),
            _const_spec(gkv.shape),
            _const_spec(wq2.shape),
            _const_spec(wk.shape),
            _const_spec(wvt.shape),
            pl.BlockSpec((4, rows, LANES), lambda b, i: (0, i, 0)),
            _const_spec(vone.shape),
        ],
        out_specs=[head_spec, head_spec, vt_spec] + dil_specs,
        out_shape=[head_shape, head_shape, vt_shape] + dil_shapes,
        scratch_shapes=[pltpu.VMEM((slabs, rows, LANES), F32),
                        pltpu.VMEM((slabs, 4, rows // 4, LANES), F32)],
        compiler_params=_params("parallel", "parallel"),
        name="in_projection",
    )(x, w_in_p, gq, gkv, wq2, wk, wvt, tables, vone)
    return outs[0], outs[1], outs[2], [outs[3 + 3 * d:6 + 3 * d] for d in range(len(DIL_PAIRS))]


def _mla_kernel(q_ref, k_ref, vt_ref, o_ref, acc_ref, m_ref, st_ref, pt_ref, *, blk):
    nq = q_ref.shape[2] // blk
    strip = 64
    unroll = 8
    m_ref[...] = jnp.full_like(m_ref, -NEG_BIG)
    acc_ref[...] = jnp.zeros_like(acc_ref)

    def rows(i):
        return pl.ds(i * blk if isinstance(i, int) else pl.multiple_of(i * blk, blk), blk)

    half = blk // 2
    assert half % strip == 0

    def half_rows(i, which):
        start = i * blk + which * half
        return pl.ds(start if isinstance(i, int) else pl.multiple_of(start, half), half)

    def scores(item, slot, masked):
        qi, j = item
        for hh in range(2):
            if not masked:
                st_ref[hh, slot] = _dot_nt(k_ref[0, hh, rows(j), :], q_ref[0, hh, rows(qi), :])
            else:
                st_ref[hh, slot, :half, :] = _dot_nt(k_ref[0, hh, half_rows(j, 0), :], q_ref[0, hh, rows(qi), :])
                st_ref[hh, slot, half:, half:] = _dot_nt(k_ref[0, hh, half_rows(j, 1), :],
                                                         q_ref[0, hh, half_rows(qi, 1), :])

    def update(item, slot, masked):
        qi, j = item

        def band(hh, s):
            if not masked:
                return st_ref[hh, slot, s * strip:(s + 1) * strip, :]
            lo = 0 if s * strip < half else half
            tile = st_ref[hh, slot, s * strip:(s + 1) * strip, lo:]
            key = lax.broadcasted_iota(jnp.int32, tile.shape, 0) + s * strip
            qry = lax.broadcasted_iota(jnp.int32, tile.shape, 1) + lo
            return jnp.where(key <= qry, tile, -NEG_BIG)

        def fold(parts):
            top = parts[0]
            for part in parts[1:]:
                top = jnp.maximum(top, part)
            return top

        for hh in range(2):
            tops = [band(hh, s).reshape(strip // 8, 8, -1).max(axis=0) for s in range(blk // strip)]
            if masked:
                n_lo = half // strip
                wide, narrow = fold(tops[:n_lo]), fold(tops[n_lo:])
                top = jnp.concatenate([wide[:, :half], jnp.maximum(wide[:, half:], narrow)], axis=1)
            else:
                top = fold(tops)
            m_old = m_ref[qi, hh]
            m_new = jnp.maximum(m_old, jnp.max(top, axis=0, keepdims=True))
            for s in range(blk // strip):
                tile = band(hh, s)
                lo = blk - tile.shape[1]
                pt_ref[hh, s * strip:(s + 1) * strip, lo:] = jnp.exp2(tile - m_new[:, lo:]).astype(BF16)
            if masked:
                vt = vt_ref[0, hh, j]
                pv = _dot(vt[:, :half], pt_ref[hh, :half, :])
                pv_hi = _dot(vt[:, half:], pt_ref[hh, half:, half:])
                pv = jnp.concatenate([pv[:, :half], pv[:, half:] + pv_hi], axis=1)
            else:
                pv = _dot(vt_ref[0, hh, j], pt_ref[hh])
            acc_ref[qi, hh] = acc_ref[qi, hh] * jnp.exp2(m_old - m_new) + pv
            m_ref[qi, hh] = m_new

    def sweep(items, advance, masked):
        n = len(items)
        peel = (n - 1) % unroll if n - 1 > unroll else n - 1
        scores(items[0], 0, masked)
        for t in range(peel):
            scores(items[t + 1], (t + 1) % 2, masked)
            update(items[t], t % 2, masked)

        def body(_, cur):
            for k in range(unroll):
                nxt = advance(cur)
                scores(nxt, (peel + k + 1) % 2, masked)
                update(cur, (peel + k) % 2, masked)
                cur = nxt
            return cur

        if peel < n - 1:
            start = (jnp.int32(items[peel][0]), jnp.int32(items[peel][1]))
            lax.fori_loop(0, (n - 1 - peel) // unroll, body, start)
        update(items[n - 1], (n - 1) % 2, masked)

    def next_below(item):
        qi, j = item
        wrap = j + 1 >= qi
        return jnp.where(wrap, qi + 1, qi), jnp.where(wrap, 0, j + 1)

    sweep([(qi, j) for qi in range(1, nq) for j in range(qi)], next_below, False)
    sweep([(qi, qi) for qi in range(nq)], lambda item: (item[0] + 1, item[1] + 1), True)
    for qi in range(nq):
        outs = []
        for hh in range(2):
            acc = acc_ref[qi, hh]
            outs.append(acc[:HEAD_DIM] / acc[HEAD_DIM:HEAD_DIM + 1])
        o_ref[0, rows(qi), :] = jnp.concatenate(outs, axis=0).T.astype(o_ref.dtype)


def _mla_attention(q, k, vt):
    B, H, S, _ = q.shape
    blk = vt.shape[-1]
    qk_spec = pl.BlockSpec((1, 2, S, LANES), lambda b, hp: (b, hp, 0, 0))
    vt_spec = pl.BlockSpec((1, 2, S // blk, LANES, blk), lambda b, hp: (b, hp, 0, 0, 0))
    return pl.pallas_call(
        functools.partial(_mla_kernel, blk=blk),
        grid=(B, H // 2),
        in_specs=[qk_spec, qk_spec, vt_spec],
        out_specs=pl.BlockSpec((1, S, LANES), lambda b, hp: (b, 0, hp)),
        out_shape=jax.ShapeDtypeStruct((B, S, H * HEAD_DIM), BF16),
        scratch_shapes=[pltpu.VMEM((S // blk, 2, LANES, blk), F32),
                        pltpu.VMEM((S // blk, 2, 1, blk), F32),
                        pltpu.VMEM((2, 2, blk, blk), F32),
                        pltpu.VMEM((2, blk, blk), BF16)],
        compiler_params=_params("parallel", "parallel"),
        name="mla_attention",
    )(q, k, vt)


def _dil_bias_tiles(bias_ref, dil):
    blk = DIL_BLOCK
    key = lax.broadcasted_iota(jnp.int32, (2 * blk, blk), 0)
    qry = lax.broadcasted_iota(jnp.int32, (2 * blk, blk), 1)
    off = qry + blk - key
    steps = jnp.where((off >= 0) & (off <= blk), off.astype(F32), NEG_BIG)
    for pair in range(DIL_HEADS // 2):
        halves = []
        for hh in range(2):
            slope = 2.0 ** (-8.0 * (2 * pair + hh + 1) / DIL_HEADS)
            halves.append(steps * (-slope * dil * math.log2(math.e)))
        tile = jnp.concatenate(halves, axis=1)
        bias_ref[0, pair] = tile
        bias_ref[1, pair, :blk, :] = jnp.full((blk, 2 * blk), -NEG_BIG, F32)
        bias_ref[1, pair, blk:, :] = tile[blk:]


def _dil_kernel(q_ref, kc_ref, kp_ref, vc_ref, vp_ref, o_ref, lse_ref, st_scr, o_scr, lse_scr, bias_ref, *tok_refs,
                dil, lane0):
    blk = DIL_BLOCK
    pairs = DIL_HEADS // 2
    nb = q_ref.shape[2] // blk
    units = dil * nb
    quarter = o_scr.shape[1] // 4
    pl.when((pl.program_id(0) == 0) & (pl.program_id(1) == 0))(functools.partial(_dil_bias_tiles, bias_ref, dil))
    first_span = pl.program_id(1) == 0
    lane = lax.broadcasted_iota(jnp.int32, (blk, LANES), 1)
    first = lane < HEAD_DIM

    def split(u):
        return u // nb, u % nb

    def row0(i):
        return i * blk

    def window(which, r, i, cols):
        cur_ref, prev_ref = ((kc_ref, kp_ref), (vc_ref, vp_ref))[which]
        if i == 0:
            return jnp.concatenate([prev_ref[0, r, :, cols], cur_ref[0, r, 0:blk, cols]], axis=0)
        return cur_ref[0, r, pl.ds(row0(i - 1), 2 * blk), cols]

    def scores(u, slot):
        r, i = split(u)
        q = q_ref[0, r, pl.ds(row0(i), blk), :]
        no_prev = first_span.astype(jnp.int32) if i == 0 else 0
        for pair in range(pairs):
            cols = slice(pair * LANES, (pair + 1) * LANES)
            qp = q[:, cols]
            zero = jnp.zeros_like(qp)
            q2 = jnp.concatenate([jnp.where(first, qp, zero), jnp.where(first, zero, qp)], axis=0)
            st_scr[slot, pair] = _dot_nt(window(0, r, i, cols), q2) + bias_ref[no_prev, pair]

    def finish(u, slot):
        r, i = split(u)
        if dil == 1:
            rows = pl.ds(row0(i), blk)
        elif tok_refs:
            rows = pl.ds((r % 4) * quarter + i * (blk * dil // 4) + r // 4, blk, stride=dil // 4)
        else:
            rows = pl.ds(r + i * (blk * dil), blk, stride=dil)
        lse_rows = []
        for pair in range(pairs):
            cols = slice(pair * LANES, (pair + 1) * LANES)
            st = st_scr[slot, pair]
            m = jnp.max(st, axis=0, keepdims=True)
            pt = jnp.exp2(st - m).astype(BF16)
            vt = jnp.concatenate([window(1, r, i, cols).T, jnp.ones((BF16_ROWS, 2 * blk), BF16)], axis=0)
            res = _dot(vt, pt)
            l = res[LANES:LANES + 1, :]
            inv = 1.0 / l
            ot = jnp.concatenate([res[:HEAD_DIM, :LANES] * inv[:, :LANES],
                                  res[HEAD_DIM:LANES, LANES:] * inv[:, LANES:]], axis=0)
            o_scr[pair, rows, :] = ot.T
            lse2 = m + jnp.log2(l)
            lse_rows += [lse2[:, :LANES], lse2[:, LANES:]]
        tile = jnp.concatenate([jnp.zeros((lane0, blk), F32)] * (lane0 > 0) + lse_rows
                               + [jnp.zeros((LANES - lane0 - DIL_HEADS, blk), F32)], axis=0)
        lse_scr[rows, :] = tile.T

    scores(0, 0)
    for u in range(units):
        if u + 1 < units:
            scores(u + 1, (u + 1) % 2)
        finish(u, u % 2)
    if tok_refs:
        o_tok, lse_tok = tok_refs
        for r4 in range(4):
            band = slice(r4 * quarter, (r4 + 1) * quarter)
            spread = pl.ds(r4, quarter, stride=4)
            for pair in range(pairs):
                o_tok[pair, spread, :] = o_scr[pair, band, :]
            lse_tok[spread, :] = lse_scr[band, :]
    else:
        o_tok, lse_tok = o_scr, lse_scr
    for pair in range(pairs):
        o_ref[0, :, pair * LANES:(pair + 1) * LANES] = o_tok[pair].astype(o_ref.dtype)
    lse_ref[0] = lse_tok[...]


def _dil_branch(q, k, v, dil, lane0):
    B, _, M, W = q.shape
    S = M * dil
    span = min(DIL_SPAN, S)
    assert span % (dil * DIL_BLOCK) == 0 and S % span == 0
    rows = span // dil
    nb = rows // DIL_BLOCK
    cur = pl.BlockSpec((1, dil, rows, W), lambda b, c: (b, 0, c, 0))
    prev = pl.BlockSpec((1, dil, DIL_BLOCK, W), lambda b, c: (b, 0, jnp.maximum(c * nb - 1, 0), 0))
    return pl.pallas_call(
        functools.partial(_dil_kernel, dil=dil, lane0=lane0),
        grid=(B, S // span),
        in_specs=[cur, cur, prev, cur, prev],
        out_specs=[pl.BlockSpec((1, span, W), lambda b, c: (b, c, 0)),
                   pl.BlockSpec((1, span, LANES), lambda b, c: (b, c, 0))],
        out_shape=[jax.ShapeDtypeStruct((B, S, W), BF16), jax.ShapeDtypeStruct((B, S, LANES), F32)],
        scratch_shapes=[pltpu.VMEM((2, DIL_HEADS // 2, 2 * DIL_BLOCK, 2 * LANES), F32),
                        pltpu.VMEM((DIL_HEADS // 2, span, LANES), F32),
                        pltpu.VMEM((span, LANES), F32),
                        pltpu.VMEM((2, DIL_HEADS // 2, 2 * DIL_BLOCK, 2 * LANES), F32)]
        + ([pltpu.VMEM((DIL_HEADS // 2, span, LANES), F32), pltpu.VMEM((span, LANES), F32)] if dil == 16 else []),
        compiler_params=_params("arbitrary", "arbitrary"),
        name=f"dilated_d{dil}",
    )(q, k, k, v, v)


def _dilated_attention(qkv_per_dil):
    outs, lses = [], []
    for i, ((window, dil), (q, k, v)) in enumerate(zip(DIL_PAIRS, qkv_per_dil)):
        assert window // dil == DIL_BLOCK
        o, lse = _dil_branch(q, k, v, dil, i * DIL_HEADS)
        outs.append(o)
        lses.append(lse)
    return outs, lses


def _layer_norm(y, g, b):
    mu = jnp.mean(y, axis=-1, keepdims=True)
    d = y - mu
    var = jnp.mean(d * d, axis=-1, keepdims=True)
    return d * lax.rsqrt(var + LN_EPS) * g + b


def _out_proj_kernel(om_ref, o0_ref, o1_ref, o2_ref, lse0_ref, lse1_ref, lse2_ref, x_ref, wo_ref, ex_ref,
                     g_ref, b_ref, o_ref):
    H, W = DIL_HEADS, DIL_WIDTH
    l0 = lse0_ref[0] + lse1_ref[0] + lse2_ref[0]
    l1 = pltpu.roll(l0, LANES - H, 1)
    l2 = pltpu.roll(l0, LANES - 2 * H, 1)
    top = jnp.maximum(l0, jnp.maximum(l1, l2))
    e0, e1, e2 = jnp.exp2(l0 - top), jnp.exp2(l1 - top), jnp.exp2(l2 - top)
    inv = 1.0 / (e0 + e1 + e2)
    lane = lax.broadcasted_iota(jnp.int32, l0.shape, 1)
    w = jnp.where(lane < H, e0 * inv,
                  jnp.where(lane < 2 * H, pltpu.roll(e1 * inv, H, 1), pltpu.roll(e2 * inv, 2 * H, 1)))
    wide = _dot(w.astype(BF16), ex_ref[...])
    o_dil = (wide[:, :W] * o0_ref[0] + wide[:, W:2 * W] * o1_ref[0] + wide[:, 2 * W:] * o2_ref[0]).astype(BF16)
    half = om_ref.shape[-1]
    mix = _dot(om_ref[0], wo_ref[:half, :]) + _dot(o_dil, wo_ref[half:, :])
    o_ref[0] = _layer_norm(DN_ALPHA * x_ref[0] + mix, g_ref[...], b_ref[...])


def _out_projection(o_mla, o_dils, lses, x, w_o, expand, g, b):
    B, S, _ = x.shape
    rows = min(OUT_ROWS, S)
    half_spec = pl.BlockSpec((1, rows, o_mla.shape[-1]), lambda bi, i: (bi, i, 0))
    lse_spec = pl.BlockSpec((1, rows, LANES), lambda bi, i: (bi, i, 0))
    x_spec = pl.BlockSpec((1, rows, D_MODEL), lambda bi, i: (bi, i, 0))
    return pl.pallas_call(
        _out_proj_kernel,
        grid=(B, S // rows),
        in_specs=[half_spec, half_spec, half_spec, half_spec, lse_spec, lse_spec, lse_spec, x_spec,
                  _const_spec(w_o.shape), _const_spec(expand.shape), _const_spec(g.shape), _const_spec(b.shape)],
        out_specs=x_spec,
        out_shape=jax.ShapeDtypeStruct(x.shape, F32),
        compiler_params=_params("parallel", "parallel"),
        name="out_projection_ln",
    )(o_mla, *o_dils, *lses, x, w_o, expand, g, b)


def _ffn_kernel(x_ref, wup_ref, cw_ref, wd_ref, g_ref, b_ref, o_ref, acc_ref, u0_ref, u1_ref, xe_ref, tail_ref):
    rows = x_ref.shape[1]
    pad = tail_ref.shape[2]
    chunks = D_FF // FFN_CHUNK

    def cols(c, part=0):
        start = c * FFN_CHUNK + part * D_FF
        return pl.ds(start if isinstance(c, int) else pl.multiple_of(start, FFN_CHUNK), FFN_CHUNK)

    @pl.when(pl.program_id(1) == 0)
    def _():
        tail_ref[...] = jnp.zeros_like(tail_ref)

    xe_ref[...] = x_ref[0].astype(BF16)
    acc_ref[...] = jnp.zeros_like(acc_ref)

    def up(c, u_ref):
        for part in range(2):
            u = _dot(xe_ref[...], wup_ref[:, cols(c, part)])
            u_ref[part, :pad, :] = tail_ref[c, part]
            u_ref[part, pad:, :] = u
            tail_ref[c, part] = u[rows - pad:, :]

    def conv(u_ref, part, taps):
        y = taps[3:4, :]
        for j in range(CONV_WIDTH):
            shift = CONV_WIDTH - 1 - j
            y = y + taps[j:j + 1, :] * u_ref[part, pl.ds(pad - shift, rows), :]
        return y

    def down(c, u_ref):
        ya = conv(u_ref, 0, cw_ref[:, cols(c, 0)])
        yg = conv(u_ref, 1, cw_ref[:, cols(c, 1)])
        c1 = math.sqrt(2.0 / math.pi)
        th = jnp.tanh(yg * (c1 + (c1 * 0.044715) * (yg * yg)))
        hidden = ((yg + yg * th) * ya).astype(BF16)
        acc_ref[...] += _dot(hidden, wd_ref[cols(c), :])

    assert chunks % 2 == 1
    up(0, u0_ref)

    def body(j, carry):
        c = 2 * j
        up(c + 1, u1_ref)
        down(c, u0_ref)
        up(c + 2, u0_ref)
        down(c + 1, u1_ref)
        return carry

    lax.fori_loop(0, chunks // 2, body, 0)
    down(chunks - 1, u0_ref)
    o_ref[0] = _layer_norm(DN_ALPHA * x_ref[0] + acc_ref[...], g_ref[...], b_ref[...])


def _ffn(x1, w_up, cw, wd, g, b):
    B, S, _ = x1.shape
    rows = min(FFN_ROWS, S)
    pad = F32_ROWS
    x_spec = pl.BlockSpec((1, rows, D_MODEL), lambda bi, i: (bi, i, 0))
    return pl.pallas_call(
        _ffn_kernel,
        grid=(B, S // rows),
        in_specs=[x_spec, _const_spec(w_up.shape), _const_spec(cw.shape),
                  _const_spec(wd.shape), _const_spec(g.shape), _const_spec(b.shape)],
        out_specs=x_spec,
        out_shape=jax.ShapeDtypeStruct(x1.shape, F32),
        scratch_shapes=[pltpu.VMEM((rows, D_MODEL), F32),
                        pltpu.VMEM((2, pad + rows, FFN_CHUNK), F32), pltpu.VMEM((2, pad + rows, FFN_CHUNK), F32),
                        pltpu.VMEM((rows, D_MODEL), BF16),
                        pltpu.VMEM((D_FF // FFN_CHUNK, 2, pad, FFN_CHUNK), F32)],
        compiler_params=_params("parallel", "arbitrary"),
        name="conv_ffn_ln",
    )(x1, w_up, cw, wd, g, b)


def _pad_cols(w, width):
    return jnp.pad(w, ((0, 0), (0, width - w.shape[1])))


def _head_groups(w):
    rank, heads, e = w.shape
    return jnp.pad(w, ((0, 0), (0, 0), (0, LANES - e))).reshape(rank, heads * LANES)


def _prepare(w_in, g_cq, g_ckv, w_uq, w_uk, w_uv, w_o, ln1_g, ln1_b, w_up, conv_w, conv_b, w_down, ln2_g, ln2_b):
    r0, r1, r2 = MLA_Q_RANK, MLA_Q_RANK + MLA_KV_RANK, MLA_Q_RANK + MLA_KV_RANK + MLA_ROPE_DIM
    w_in_p = jnp.concatenate([w_in[:, :r1], _pad_cols(w_in[:, r1:r2], LANES), w_in[:, r2:]], axis=1).astype(BF16)

    half = MLA_ROPE_DIM // 2
    rope = w_uq[:, :, MLA_NOPE_DIM:]
    swapped = jnp.concatenate([-rope[:, :, half:], rope[:, :, :half]], axis=-1)
    swapped = jnp.concatenate([jnp.zeros_like(w_uq[:, :, :MLA_NOPE_DIM]), swapped], axis=-1)
    wq2 = jnp.concatenate([_head_groups(w_uq), _head_groups(swapped)], axis=1).astype(BF16)

    wk = _head_groups(w_uk).astype(BF16)
    wvt = jnp.pad(w_uv.transpose(1, 2, 0), ((0, 0), (0, LANES - HEAD_DIM), (0, 0))).astype(BF16)
    vone = jnp.zeros((LANES, 1), F32).at[HEAD_DIM, 0].set(1.0)

    assert D_FF % FFN_CHUNK == 0
    taps = jnp.concatenate([conv_w, conv_b[None, :]], axis=0)
    cw = jnp.concatenate([0.5 * taps[:, :D_FF], taps[:, D_FF:]], axis=1)
    src = jnp.arange(LANES)[:, None]
    dst = jnp.arange(len(DIL_PAIRS) * DIL_WIDTH)[None, :]
    expand = (src == dst // HEAD_DIM).astype(BF16)
    row = lambda a: a.reshape(1, -1)
    return dict(w_in_p=w_in_p, gq=row(g_cq), gkv=row(g_ckv), wq2=wq2, wk=wk, wvt=wvt, vone=vone,
                w_o=w_o.astype(BF16), expand=expand, ln1=(row(ln1_g), row(ln1_b)), w_up=w_up.astype(BF16), cw=cw,
                wd=w_down.astype(BF16),
                ln2=(row(ln2_g), row(ln2_b)))


def _freq_lanes():
    half = MLA_ROPE_DIM // 2
    freqs = ROPE_THETA ** (-jnp.arange(half, dtype=F32) / half)
    zeros = jnp.zeros((MLA_NOPE_DIM,), F32)
    return jnp.concatenate([zeros, freqs, freqs, jnp.zeros((LANES - MLA_NOPE_DIM - MLA_ROPE_DIM,), F32)])[None, :]


def kernel(x, w_in, g_cq, g_ckv, w_uq, w_uk, w_uv, w_o, ln1_g, ln1_b, w_up, conv_w, conv_b, w_down, ln2_g, ln2_b):
    B, S, _ = x.shape
    p = _prepare(w_in, g_cq, g_ckv, w_uq, w_uk, w_uv, w_o, ln1_g, ln1_b, w_up, conv_w, conv_b, w_down, ln2_g, ln2_b)
    scale = math.log2(math.e) / math.sqrt(MLA_NOPE_DIM + MLA_ROPE_DIM)
    tables = _rope_tables(S, _freq_lanes(), scale)
    q, k, vt, qkv_dil = _projection(x, p["w_in_p"], p["gq"], p["gkv"], p["wq2"], p["wk"], p["wvt"], tables,
                                    p["vone"])
    o_mla = _mla_attention(q, k, vt)
    o_dils, lses = _dilated_attention(qkv_dil)
    x1 = _out_projection(o_mla, o_dils, lses, x, p["w_o"], p["expand"], *p["ln1"])
    return _ffn(x1, p["w_up"], p["cw"], p["wd"], *p["ln2"])
```
